```python
import jax, jax.numpy as jnp
from jax import lax
import numpy as np

D_MODEL = 1024
BATCH = 8
SEQ = 16384
DEPTH = 4

HEAD_DIM = 64
A_Q_HEADS = 8
A_KV_HEADS = 2
A_WINDOW = 128
B_GROUPS = ((128, 1), (512, 4), (2048, 16))
B_HEADS_PER_GROUP = 4
B_HEADS = B_HEADS_PER_GROUP * len(B_GROUPS)
N_ATTN_HEADS = A_Q_HEADS + B_HEADS
BLOCK = 128
A_Q_W = A_Q_HEADS * HEAD_DIM
A_KV_W = A_KV_HEADS * HEAD_DIM
B_W = B_HEADS * HEAD_DIM
B_OUT_W = B_HEADS_PER_GROUP * HEAD_DIM
IN_SPLITS = (A_Q_W, A_KV_W, A_KV_W, B_W, B_W, B_W, D_MODEL, D_MODEL)
IN_W = sum(IN_SPLITS)
D_FF = ((8 * D_MODEL + 3 * 256 - 1) // (3 * 256)) * 256
DN_ALPHA = (2 * DEPTH) ** 0.25
DN_BETA = (8 * DEPTH) ** -0.25
LN_EPS = 1e-5
NEG_INF = -1e30

kernel_name = "hybrid_swa_sink_dilated_gated_deepnorm"


def layer_norm(x, g, b):
    xf = x.astype(jnp.float32)
    mu = xf.mean(-1, keepdims=True)
    var = jnp.square(xf - mu).mean(-1, keepdims=True)
    y = (xf - mu) * lax.rsqrt(var + LN_EPS)
    return (y * g.astype(jnp.float32) + b.astype(jnp.float32)).astype(x.dtype)


def alibi_slopes(n):
    return jnp.exp2(-8.0 * jnp.arange(1, n + 1, dtype=jnp.float32) / n)


def banded_attention(q, k, v, slopes, max_dist, stride, sinks=None):
    bt, L, H, dh = q.shape
    hkv = k.shape[2]
    G = H // hkv
    nb = -(-L // BLOCK)
    Lp = nb * BLOCK
    q = jnp.pad(q, ((0, 0), (0, Lp - L), (0, 0), (0, 0)))
    kv_pad = ((0, 0), (BLOCK, Lp - L), (0, 0), (0, 0))
    k = jnp.pad(k, kv_pad).reshape(bt, nb + 1, BLOCK, hkv, dh)
    v = jnp.pad(v, kv_pad).reshape(bt, nb + 1, BLOCK, hkv, dh)
    kw = jnp.concatenate([k[:, :-1], k[:, 1:]], axis=2)
    vw = jnp.concatenate([v[:, :-1], v[:, 1:]], axis=2)
    qb = q.reshape(bt, nb, BLOCK, hkv, G, dh)
    s = jnp.einsum('bnqhgd,bnshd->bnhgqs', qb, kw,
                   preferred_element_type=jnp.float32) * (dh ** -0.5)
    qi = jnp.arange(BLOCK)[:, None]
    sj = jnp.arange(2 * BLOCK)[None, :]
    dist = qi + BLOCK - sj
    kpos = jnp.arange(nb)[:, None] * BLOCK + jnp.arange(2 * BLOCK)[None, :] - BLOCK
    valid = ((dist >= 0) & (dist <= max_dist))[None] & (kpos >= 0)[:, None, :]
    bias = -(slopes.astype(jnp.float32).reshape(hkv, G, 1, 1)
             * (dist * stride).astype(jnp.float32))
    s = jnp.where(valid[None, :, None, None], s + bias, NEG_INF)
    m = s.max(-1)
    if sinks is not None:
        sink = sinks.astype(jnp.float32).reshape(1, 1, hkv, G, 1)
        m = jnp.maximum(m, sink)
    e = jnp.exp(s - m[..., None])
    den = e.sum(-1)
    if sinks is not None:
        den = den + jnp.exp(sink - m)
    lse = m + jnp.log(den)
    p = (e / den[..., None]).astype(v.dtype)
    o = jnp.einsum('bnhgqs,bnshd->bnqhgd', p, vw).reshape(bt, Lp, H, dh)[:, :L]
    lse = lse.transpose(0, 1, 4, 2, 3).reshape(bt, Lp, H)[:, :L]
    return o, lse


def dilated_group(q, k, v, slopes, window, dilation):
    b, S, h, dh = q.shape
    n = S // dilation

    def fold(t):
        return t.reshape(b, n, dilation, h, dh).transpose(0, 2, 1, 3, 4).reshape(b * dilation, n, h, dh)

    o, lse = banded_attention(fold(q), fold(k), fold(v), slopes, window // dilation, dilation)
    o = o.reshape(b, dilation, n, h, dh).transpose(0, 2, 1, 3, 4).reshape(b, S, h, dh)
    lse = lse.reshape(b, dilation, n, h).transpose(0, 2, 1, 3).reshape(b, S, h)
    return o, lse


def token_mixer(u, w_in, sinks, w_a, w_b, w_o):
    b, S, _ = u.shape
    idx = list(np.cumsum(IN_SPLITS)[:-1])
    qa, ka, va, qb, kb, vb, ga, gb = jnp.split(u @ w_in, idx, axis=-1)
    slopes = alibi_slopes(N_ATTN_HEADS)
    ya, _ = banded_attention(qa.reshape(b, S, A_Q_HEADS, HEAD_DIM),
                             ka.reshape(b, S, A_KV_HEADS, HEAD_DIM),
                             va.reshape(b, S, A_KV_HEADS, HEAD_DIM),
                             slopes[:A_Q_HEADS], A_WINDOW - 1, 1, sinks)
    ya = ya.reshape(b, S, A_Q_W)
    gshape = (b, S, len(B_GROUPS), B_HEADS_PER_GROUP, HEAD_DIM)
    qb, kb, vb = qb.reshape(gshape), kb.reshape(gshape), vb.reshape(gshape)
    outs, lses = [], []
    for g, (window, dilation) in enumerate(B_GROUPS):
        lo = A_Q_HEADS + g * B_HEADS_PER_GROUP
        o, l = dilated_group(qb[:, :, g], kb[:, :, g], vb[:, :, g],
                             slopes[lo:lo + B_HEADS_PER_GROUP], window, dilation)
        outs.append(o)
        lses.append(l)
    wts = jax.nn.softmax(jnp.stack(lses), axis=0)
    yb = (jnp.stack(outs) * wts[..., None].astype(u.dtype)).sum(0).reshape(b, S, B_OUT_W)
    merged = jax.nn.sigmoid(ga) * (ya @ w_a) + jax.nn.sigmoid(gb) * (yb @ w_b)
    return merged @ w_o


def swiglu(u, w_gate, w_up, w_down):
    return (jax.nn.silu(u @ w_gate) * (u @ w_up)) @ w_down


def _fwd_setup_inputs(seed: int = 0) -> dict:
    key = jax.random.key(seed)
    ks = jax.random.split(key, 20)
    nrm = lambda k, shape, s: jax.random.normal(k, shape, jnp.float32) * s
    L, D = DEPTH, D_MODEL
    return {
        "x": nrm(ks[0], (BATCH, SEQ, D), 1.0),
        "c": nrm(ks[1], (BATCH, D), 1.0),
        "w_ada": nrm(ks[2], (L, D, 6 * D), 0.5 * D ** -0.5),
        "b_ada": nrm(ks[3], (L, 6 * D), 0.02),
        "w_in": nrm(ks[4], (L, D, IN_W), D ** -0.5),
        "sinks": nrm(ks[5], (L, A_Q_HEADS), 0.5),
        "w_a": nrm(ks[6], (L, A_Q_W, D), A_Q_W ** -0.5),
        "w_b": nrm(ks[7], (L, B_OUT_W, D), B_OUT_W ** -0.5),
        "w_o": nrm(ks[8], (L, D, D), DN_BETA * D ** -0.5),
        "ln1_g": 1.0 + nrm(ks[9], (L, D), 0.02),
        "ln1_b": nrm(ks[10], (L, D), 0.02),
        "w_gate": nrm(ks[11], (L, D, D_FF), D ** -0.5),
        "w_up": nrm(ks[12], (L, D, D_FF), D ** -0.5),
        "w_down": nrm(ks[13], (L, D_FF, D), DN_BETA * D_FF ** -0.5),
        "ln2_g": 1.0 + nrm(ks[14], (L, D), 0.02),
        "ln2_b": nrm(ks[15], (L, D), 0.02),
    }


def _fwd_reference(x, c, w_ada, b_ada, w_in, sinks, w_a, w_b, w_o, ln1_g, ln1_b,
              w_gate, w_up, w_down, ln2_g, ln2_b):
    sc = jax.nn.silu(c)
    for l in range(DEPTH):
        mod = (sc @ w_ada[l] + b_ada[l])[:, None, :]
        sh1, s1, g1, sh2, s2, g2 = jnp.split(mod, 6, axis=-1)
        u = x * (1 + s1) + sh1
        x = layer_norm(DN_ALPHA * x + g1 * token_mixer(u, w_in[l], sinks[l], w_a[l], w_b[l], w_o[l]),
                       ln1_g[l], ln1_b[l])
        u = x * (1 + s2) + sh2
        x = layer_norm(DN_ALPHA * x + g2 * swiglu(u, w_gate[l], w_up[l], w_down[l]),
                       ln2_g[l], ln2_b[l])
    return x


import jax as _jax
import jax.numpy as _jnp

TWIN_FORMAT = 'train_step'
FWD_PARAMS = ['x', 'c', 'w_ada', 'b_ada', 'w_in', 'sinks', 'w_a', 'w_b', 'w_o', 'ln1_g', 'ln1_b', 'w_gate', 'w_up', 'w_down', 'ln2_g', 'ln2_b']
TWIN_WEIGHTS = ['w_ada', 'b_ada', 'w_in', 'sinks', 'w_a', 'w_b', 'w_o', 'ln1_g', 'ln1_b', 'w_gate', 'w_up', 'w_down', 'ln2_g', 'ln2_b']
TWIN_DIFF_INPUT = 'x'
TWIN_INPUTS = ['x', 'c', 'w_ada', 'b_ada', 'w_in', 'sinks', 'w_a', 'w_b', 'w_o', 'ln1_g', 'ln1_b', 'w_gate', 'w_up', 'w_down', 'ln2_g', 'ln2_b', 'loss_target', 'm_w_ada', 'm_b_ada', 'm_w_in', 'm_sinks', 'm_w_a', 'm_w_b', 'm_w_o', 'm_ln1_g', 'm_ln1_b', 'm_w_gate', 'm_w_up', 'm_w_down', 'm_ln2_g', 'm_ln2_b', 'v_w_ada', 'v_b_ada', 'v_w_in', 'v_sinks', 'v_w_a', 'v_w_b', 'v_w_o', 'v_ln1_g', 'v_ln1_b', 'v_w_gate', 'v_w_up', 'v_w_down', 'v_ln2_g', 'v_ln2_b']
TWIN_OUTPUTS = ['loss', 'grad_x', 'grad_w_ada', 'grad_b_ada', 'grad_w_in', 'grad_sinks', 'grad_w_a', 'grad_w_b', 'grad_w_o', 'grad_ln1_g', 'grad_ln1_b', 'grad_w_gate', 'grad_w_up', 'grad_w_down', 'grad_ln2_g', 'grad_ln2_b', 'delta_w_ada', 'delta_b_ada', 'delta_w_in', 'delta_sinks', 'delta_w_a', 'delta_w_b', 'delta_w_o', 'delta_ln1_g', 'delta_ln1_b', 'delta_w_gate', 'delta_w_up', 'delta_w_down', 'delta_ln2_g', 'delta_ln2_b', 'new_m_w_ada', 'new_m_b_ada', 'new_m_w_in', 'new_m_sinks', 'new_m_w_a', 'new_m_w_b', 'new_m_w_o', 'new_m_ln1_g', 'new_m_ln1_b', 'new_m_w_gate', 'new_m_w_up', 'new_m_w_down', 'new_m_ln2_g', 'new_m_ln2_b', 'new_v_w_ada', 'new_v_b_ada', 'new_v_w_in', 'new_v_sinks', 'new_v_w_a', 'new_v_w_b', 'new_v_w_o', 'new_v_ln1_g', 'new_v_ln1_b', 'new_v_w_gate', 'new_v_w_up', 'new_v_w_down', 'new_v_ln2_g', 'new_v_ln2_b']
TWIN_LEAF_KINDS = {'loss': 'loss', 'grad_x': 'grad_x', 'grad_w_ada': 'grad_w', 'grad_b_ada': 'grad_w', 'grad_w_in': 'grad_w', 'grad_sinks': 'grad_w', 'grad_w_a': 'grad_w', 'grad_w_b': 'grad_w', 'grad_w_o': 'grad_w', 'grad_ln1_g': 'grad_w', 'grad_ln1_b': 'grad_w', 'grad_w_gate': 'grad_w', 'grad_w_up': 'grad_w', 'grad_w_down': 'grad_w', 'grad_ln2_g': 'grad_w', 'grad_ln2_b': 'grad_w', 'delta_w_ada': 'delta_w', 'delta_b_ada': 'delta_w', 'delta_w_in': 'delta_w', 'delta_sinks': 'delta_w', 'delta_w_a': 'delta_w', 'delta_w_b': 'delta_w', 'delta_w_o': 'delta_w', 'delta_ln1_g': 'delta_w', 'delta_ln1_b': 'delta_w', 'delta_w_gate': 'delta_w', 'delta_w_up': 'delta_w', 'delta_w_down': 'delta_w', 'delta_ln2_g': 'delta_w', 'delta_ln2_b': 'delta_w', 'new_m_w_ada': 'new_m', 'new_m_b_ada': 'new_m', 'new_m_w_in': 'new_m', 'new_m_sinks': 'new_m', 'new_m_w_a': 'new_m', 'new_m_w_b': 'new_m', 'new_m_w_o': 'new_m', 'new_m_ln1_g': 'new_m', 'new_m_ln1_b': 'new_m', 'new_m_w_gate': 'new_m', 'new_m_w_up': 'new_m', 'new_m_w_down': 'new_m', 'new_m_ln2_g': 'new_m', 'new_m_ln2_b': 'new_m', 'new_v_w_ada': 'new_v', 'new_v_b_ada': 'new_v', 'new_v_w_in': 'new_v', 'new_v_sinks': 'new_v', 'new_v_w_a': 'new_v', 'new_v_w_b': 'new_v', 'new_v_w_o': 'new_v', 'new_v_ln1_g': 'new_v', 'new_v_ln1_b': 'new_v', 'new_v_w_gate': 'new_v', 'new_v_w_up': 'new_v', 'new_v_w_down': 'new_v', 'new_v_ln2_g': 'new_v', 'new_v_ln2_b': 'new_v'}


def _forward(args):
    return _fwd_reference(*[args[k] for k in FWD_PARAMS])


def _output_shape():
    def fwd():
        inp = _fwd_setup_inputs(0)
        return _fwd_reference(*[inp[k] for k in FWD_PARAMS])
    out = _jax.eval_shape(fwd)
    return out.shape, out.dtype

N_MICROBATCH = 1
ADAM_LR = 0.001
ADAM_B1 = 0.9
ADAM_B2 = 0.999
ADAM_EPS = 1e-08
ADAM_WD = 0.01
ADAM_STEP = 10
PER_EXAMPLE_BATCH_AXIS = {'x': 0, 'c': 0, 'loss_target': 0}
SHARED_INPUTS = []
_WEIGHT_DTYPES = {'w_ada': _jnp.float32, 'b_ada': _jnp.float32, 'w_in': _jnp.float32, 'sinks': _jnp.float32, 'w_a': _jnp.float32, 'w_b': _jnp.float32, 'w_o': _jnp.float32, 'ln1_g': _jnp.float32, 'ln1_b': _jnp.float32, 'w_gate': _jnp.float32, 'w_up': _jnp.float32, 'w_down': _jnp.float32, 'ln2_g': _jnp.float32, 'ln2_b': _jnp.float32}
MOMENT_SCALE = {'w_ada': 2.085004e-02, 'b_ada': 3.555114e-02, 'w_in': 6.024140e-03, 'sinks': 1.571225e-02, 'w_a': 8.127218e-03, 'w_b': 5.073127e-03, 'w_o': 2.146616e-02, 'ln1_g': 4.496814e+00, 'ln1_b': 1.734307e+00, 'w_gate': 1.211996e-02, 'w_up': 1.177132e-02, 'w_down': 4.642372e-02, 'ln2_g': 6.451326e+01, 'ln2_b': 3.242216e+00}


def _to_microbatches(a, axis):
    t = _jnp.moveaxis(a, axis, 0)
    t = t.reshape((N_MICROBATCH, t.shape[0] // N_MICROBATCH) + t.shape[1:])
    return _jnp.moveaxis(t, 1, axis + 1)


def setup_inputs(seed: int = 0) -> dict:
    inp = _fwd_setup_inputs(seed)
    key = _jax.random.fold_in(_jax.random.key(seed), 7919)
    shape, _ = _output_shape()
    out = dict(inp)
    out["loss_target"] = _jax.random.normal(_jax.random.fold_in(key, 0), shape, _jnp.float32)
    for i, name in enumerate(TWIN_WEIGHTS):
        w = inp[name].astype(_jnp.float32)
        if MOMENT_SCALE is None:
            s = _jnp.sqrt(_jnp.mean(_jnp.square(w)) + 1e-30)
        else:
            s = MOMENT_SCALE[name]
        km, kv = _jax.random.split(_jax.random.fold_in(key, i + 1))
        out[name] = w
        out["m_" + name] = s * _jax.random.normal(km, w.shape, _jnp.float32)
        out["v_" + name] = (s * s) * _jax.random.uniform(kv, w.shape, _jnp.float32, 0.5, 1.5)
    if N_MICROBATCH > 1:
        for name, axis in PER_EXAMPLE_BATCH_AXIS.items():
            out[name] = _to_microbatches(out[name], axis)
    return {'x': out['x'], 'c': out['c'], 'w_ada': out['w_ada'], 'b_ada': out['b_ada'], 'w_in': out['w_in'], 'sinks': out['sinks'], 'w_a': out['w_a'], 'w_b': out['w_b'], 'w_o': out['w_o'], 'ln1_g': out['ln1_g'], 'ln1_b': out['ln1_b'], 'w_gate': out['w_gate'], 'w_up': out['w_up'], 'w_down': out['w_down'], 'ln2_g': out['ln2_g'], 'ln2_b': out['ln2_b'], 'loss_target': out['loss_target'], 'm_w_ada': out['m_w_ada'], 'm_b_ada': out['m_b_ada'], 'm_w_in': out['m_w_in'], 'm_sinks': out['m_sinks'], 'm_w_a': out['m_w_a'], 'm_w_b': out['m_w_b'], 'm_w_o': out['m_w_o'], 'm_ln1_g': out['m_ln1_g'], 'm_ln1_b': out['m_ln1_b'], 'm_w_gate': out['m_w_gate'], 'm_w_up': out['m_w_up'], 'm_w_down': out['m_w_down'], 'm_ln2_g': out['m_ln2_g'], 'm_ln2_b': out['m_ln2_b'], 'v_w_ada': out['v_w_ada'], 'v_b_ada': out['v_b_ada'], 'v_w_in': out['v_w_in'], 'v_sinks': out['v_sinks'], 'v_w_a': out['v_w_a'], 'v_w_b': out['v_w_b'], 'v_w_o': out['v_w_o'], 'v_ln1_g': out['v_ln1_g'], 'v_ln1_b': out['v_ln1_b'], 'v_w_gate': out['v_w_gate'], 'v_w_up': out['v_w_up'], 'v_w_down': out['v_w_down'], 'v_ln2_g': out['v_ln2_g'], 'v_ln2_b': out['v_ln2_b']}


def _loss(weights, diff, rest, loss_target):
    with _jax.named_scope("forward"):
        args = {**rest, TWIN_DIFF_INPUT: diff, **{k: w.astype(_WEIGHT_DTYPES[k]) for k, w in weights.items()}}
        y = _forward(args)
    with _jax.named_scope("loss_head"):
        err = _jnp.square(y.astype(_jnp.float32) - loss_target)
        return 0.5 * _jnp.sum(_jnp.mean(err, axis=-1)) if err.ndim else 0.5 * err


def _adamw(w, g, m, v):
    m = ADAM_B1 * m + (1.0 - ADAM_B1) * g
    v = ADAM_B2 * v + (1.0 - ADAM_B2) * _jnp.square(g)
    m_hat = m / (1.0 - ADAM_B1 ** ADAM_STEP)
    v_hat = v / (1.0 - ADAM_B2 ** ADAM_STEP)
    delta = -ADAM_LR * (m_hat / (_jnp.sqrt(v_hat) + ADAM_EPS) + ADAM_WD * w)
    return delta, m, v


def reference(x, c, w_ada, b_ada, w_in, sinks, w_a, w_b, w_o, ln1_g, ln1_b, w_gate, w_up, w_down, ln2_g, ln2_b, loss_target, m_w_ada, m_b_ada, m_w_in, m_sinks, m_w_a, m_w_b, m_w_o, m_ln1_g, m_ln1_b, m_w_gate, m_w_up, m_w_down, m_ln2_g, m_ln2_b, v_w_ada, v_b_ada, v_w_in, v_sinks, v_w_a, v_w_b, v_w_o, v_ln1_g, v_ln1_b, v_w_gate, v_w_up, v_w_down, v_ln2_g, v_ln2_b):
    given = dict(x=x, c=c, w_ada=w_ada, b_ada=b_ada, w_in=w_in, sinks=sinks, w_a=w_a, w_b=w_b, w_o=w_o, ln1_g=ln1_g, ln1_b=ln1_b, w_gate=w_gate, w_up=w_up, w_down=w_down, ln2_g=ln2_g, ln2_b=ln2_b, loss_target=loss_target, m_w_ada=m_w_ada, m_b_ada=m_b_ada, m_w_in=m_w_in, m_sinks=m_sinks, m_w_a=m_w_a, m_w_b=m_w_b, m_w_o=m_w_o, m_ln1_g=m_ln1_g, m_ln1_b=m_ln1_b, m_w_gate=m_w_gate, m_w_up=m_w_up, m_w_down=m_w_down, m_ln2_g=m_ln2_g, m_ln2_b=m_ln2_b, v_w_ada=v_w_ada, v_b_ada=v_b_ada, v_w_in=v_w_in, v_sinks=v_sinks, v_w_a=v_w_a, v_w_b=v_w_b, v_w_o=v_w_o, v_ln1_g=v_ln1_g, v_ln1_b=v_ln1_b, v_w_gate=v_w_gate, v_w_up=v_w_up, v_w_down=v_w_down, v_ln2_g=v_ln2_g, v_ln2_b=v_ln2_b)
    weights = {n: given[n] for n in TWIN_WEIGHTS}
    shared = {n: given[n] for n in SHARED_INPUTS}
    per_example = {n: given[n] for n in ['x', 'c']}
    grad_fn = _jax.value_and_grad(_loss, argnums=(0, 1))

    def one_microbatch(ex, loss_target):
        ex = dict(ex)
        diff = ex.pop(TWIN_DIFF_INPUT)
        return grad_fn(weights, diff, {**shared, **ex}, loss_target)

    if N_MICROBATCH == 1:
        loss, (grad_w, grad_x) = one_microbatch(per_example, given["loss_target"])
    else:
        def body(carry, xs):
            loss_sum, grad_sum = carry
            l_k, (gw_k, gx_k) = one_microbatch(xs[0], xs[1])
            with _jax.named_scope("update"):
                return (loss_sum + l_k, _jax.tree.map(_jnp.add, grad_sum, gw_k)), gx_k

        init = (_jnp.zeros((), _jnp.float32), _jax.tree.map(_jnp.zeros_like, weights))
        (loss, grad_w), grad_x = _jax.lax.scan(body, init, (per_example, given["loss_target"]))
    with _jax.named_scope("update"):
        delta_w, new_m, new_v = {}, {}, {}
        for n in TWIN_WEIGHTS:
            delta_w[n], new_m[n], new_v[n] = _adamw(weights[n], grad_w[n], given["m_" + n], given["v_" + n])
    return (loss, grad_x, *[grad_w[n] for n in TWIN_WEIGHTS], *[delta_w[n] for n in TWIN_WEIGHTS],
            *[new_m[n] for n in TWIN_WEIGHTS], *[new_v[n] for n in TWIN_WEIGHTS])
```

```python
import functools

import numpy as np
import jax
import jax.numpy as jnp
from jax import lax
from jax.experimental import pallas as pl
from jax.experimental.pallas import tpu as pltpu

F32 = jnp.float32
BF16 = jnp.bfloat16
MESH = pl.DeviceIdType.MESH

D = 1024
DEPTH = 4
HD = 64
A_H = 8
A_HKV = 2
A_WINDOW = 128
B_GROUPS = ((128, 1), (512, 4), (2048, 16))
B_HG = 4
N_HEADS = A_H + B_HG * len(B_GROUPS)
BLK = 128
A_QW = A_H * HD
A_KW = A_HKV * HD
B_W = B_HG * len(B_GROUPS) * HD
B_OW = B_HG * HD
QKV_W = A_QW + 2 * A_KW + 3 * B_W
IN_W = QKV_W + 2 * D
D_FF = 2816
N_CHIPS = 4
N_DEV = 8
FF_S = D_FF // N_CHIPS
IN_S = IN_W // N_CHIPS
ALPHA = (2 * DEPTH) ** 0.25
LN_EPS = 1e-5
NEG_INF = -1e30
ADAM_LR = 0.001
ADAM_B1 = 0.9
ADAM_B2 = 0.999
ADAM_EPS = 1e-08
ADAM_WD = 0.01
ADAM_STEP = 10
SMALL_ROWS = 48
VMEM_LIMIT = 56 * 1024 * 1024


def _cp(sem):
    return pltpu.CompilerParams(dimension_semantics=sem, vmem_limit_bytes=VMEM_LIMIT)


def _nt(a, b):
    return lax.dot_general(a, b, (((1,), (1,)), ((), ())), preferred_element_type=F32)


def _tn(a, b):
    return lax.dot_general(a, b, (((0,), (0,)), ((), ())), preferred_element_type=F32)


def _nn(a, b):
    return jnp.dot(a, b, preferred_element_type=F32)


def _sigmoid(x):
    return jax.nn.sigmoid(x)


def _layer_norm_rows(r, g, b):
    mu = jnp.mean(r, axis=-1, keepdims=True)
    cen = r - mu
    var = jnp.mean(cen * cen, axis=-1, keepdims=True)
    rstd = lax.rsqrt(var + LN_EPS)
    xhat = cen * rstd
    return xhat, rstd, xhat * g + b


def _layer_norm_bwd_rows(dy, xhat, rstd, g):
    dxh = dy * g
    m1 = jnp.mean(dxh, axis=-1, keepdims=True)
    m2 = jnp.mean(dxh * xhat, axis=-1, keepdims=True)
    return rstd * (dxh - m1 - xhat * m2)


def _colsum(v):
    return jnp.sum(v, axis=0, keepdims=True)


def _my_pos():
    return lax.axis_index("x"), lax.axis_index("y"), lax.axis_index("c")


def _flip(v, bit):
    return 1 - v if bit else v


def _all_gather_small(v):
    rows, cols = v.shape

    def body(v_ref, o_ref, send_sems, recv_sems):
        x, y, c = _my_pos()
        me = 4 * x + 2 * y + c
        o_ref[me] = v_ref[...]
        peers = []
        for k in range(1, N_DEV):
            peers.append((_flip(x, k & 4), _flip(y, k & 2), _flip(c, k & 1)))
        sends = []
        for k, peer in enumerate(peers):
            cp = pltpu.make_async_remote_copy(
                src_ref=v_ref, dst_ref=o_ref.at[me], send_sem=send_sems.at[k], recv_sem=recv_sems.at[k],
                device_id=peer, device_id_type=MESH)
            cp.start()
            sends.append(cp)
        for k, (px, py, pc) in enumerate(peers):
            pltpu.make_async_remote_copy(
                src_ref=v_ref, dst_ref=o_ref.at[4 * px + 2 * py + pc], send_sem=send_sems.at[k],
                recv_sem=recv_sems.at[k], device_id=(px, py, pc), device_id_type=MESH).wait_recv()
        for cp in sends:
            cp.wait_send()

    return pl.pallas_call(
        body,
        out_shape=jax.ShapeDtypeStruct((N_DEV, rows, cols), F32),
        in_specs=[pl.BlockSpec(memory_space=pltpu.VMEM)],
        out_specs=pl.BlockSpec(memory_space=pltpu.VMEM),
        scratch_shapes=[pltpu.SemaphoreType.DMA((N_DEV - 1,)), pltpu.SemaphoreType.DMA((N_DEV - 1,))],
        name="all_gather_small",
    )(v)


def _chip_peers(x, y):
    return [(_flip(x, k & 2), _flip(y, k & 1)) for k in (1, 2, 3)]


def _gather_weights(shards):
    n = len(shards)

    def body(*refs):
        ins, outs = refs[:n], refs[n:2 * n]
        send_sems, recv_sems, loc_sems = refs[2 * n:]
        x, y, c = _my_pos()
        me = 2 * x + y
        local = []
        for w in range(n):
            cp = pltpu.make_async_copy(ins[w], outs[w].at[me], loc_sems.at[w])
            cp.start()
            local.append(cp)
        peers = _chip_peers(x, y)
        sends = []
        for k, (px, py) in enumerate(peers):
            for w in range(n):
                cp = pltpu.make_async_remote_copy(
                    src_ref=ins[w], dst_ref=outs[w].at[me], send_sem=send_sems.at[3 * w + k],
                    recv_sem=recv_sems.at[3 * w + k], device_id=(px, py, c), device_id_type=MESH)
                cp.start()
                sends.append(cp)
        for k, (px, py) in enumerate(peers):
            for w in range(n):
                pltpu.make_async_remote_copy(
                    src_ref=ins[w], dst_ref=outs[w].at[2 * px + py], send_sem=send_sems.at[3 * w + k],
                    recv_sem=recv_sems.at[3 * w + k], device_id=(px, py, c), device_id_type=MESH).wait_recv()
        for cp in sends:
            cp.wait_send()
        for cp in local:
            cp.wait()

    return pl.pallas_call(
        body,
        out_shape=[jax.ShapeDtypeStruct((N_CHIPS,) + s.shape, s.dtype) for s in shards],
        in_specs=[pl.BlockSpec(memory_space=pl.ANY)] * n,
        out_specs=[pl.BlockSpec(memory_space=pl.ANY)] * n,
        scratch_shapes=[pltpu.SemaphoreType.DMA((3 * n,)), pltpu.SemaphoreType.DMA((3 * n,)),
                        pltpu.SemaphoreType.DMA((n,))],
        name="gather_weights",
    )(*shards)


def _scatter_grads(grads):
    n = len(grads)

    def body(*refs):
        ins, outs = refs[:n], refs[n:2 * n]
        send_sems, recv_sems, loc_sems = refs[2 * n:]
        x, y, c = _my_pos()
        me = 2 * x + y
        local = []
        for w in range(n):
            cp = pltpu.make_async_copy(ins[w].at[:, me], outs[w].at[me], loc_sems.at[w])
            cp.start()
            local.append(cp)
        peers = _chip_peers(x, y)
        sends = []
        for k, (px, py) in enumerate(peers):
            for w in range(n):
                cp = pltpu.make_async_remote_copy(
                    src_ref=ins[w].at[:, 2 * px + py], dst_ref=outs[w].at[me], send_sem=send_sems.at[3 * w + k],
                    recv_sem=recv_sems.at[3 * w + k], device_id=(px, py, c), device_id_type=MESH)
                cp.start()
                sends.append(cp)
        for k, (px, py) in enumerate(peers):
            for w in range(n):
                pltpu.make_async_remote_copy(
                    src_ref=ins[w].at[:, me], dst_ref=outs[w].at[2 * px + py], send_sem=send_sems.at[3 * w + k],
                    recv_sem=recv_sems.at[3 * w + k], device_id=(px, py, c), device_id_type=MESH).wait_recv()
        for cp in sends:
            cp.wait_send()
        for cp in local:
            cp.wait()

    outs = []
    for g in grads:
        nl, nj, k, m = g.shape
        outs.append(jax.ShapeDtypeStruct((nj, nl, k, m), g.dtype))
    return pl.pallas_call(
        body,
        out_shape=outs,
        in_specs=[pl.BlockSpec(memory_space=pl.ANY)] * n,
        out_specs=[pl.BlockSpec(memory_space=pl.ANY)] * n,
        scratch_shapes=[pltpu.SemaphoreType.DMA((3 * n,)), pltpu.SemaphoreType.DMA((3 * n,)),
                        pltpu.SemaphoreType.DMA((n,))],
        name="scatter_grads",
    )(*grads)


def _swap_with_sibling(parts):
    n = len(parts)

    def body(*refs):
        ins, outs = refs[:n], refs[n:2 * n]
        send_sems, recv_sems = refs[2 * n:]
        x, y, c = _my_pos()
        sib = (x, y, 1 - c)
        cps = []
        for w in range(n):
            cp = pltpu.make_async_remote_copy(
                src_ref=ins[w], dst_ref=outs[w], send_sem=send_sems.at[w], recv_sem=recv_sems.at[w],
                device_id=sib, device_id_type=MESH)
            cp.start()
            cps.append(cp)
        for cp in cps:
            cp.wait_recv()
        for cp in cps:
            cp.wait_send()

    return pl.pallas_call(
        body,
        out_shape=[jax.ShapeDtypeStruct(p.shape, p.dtype) for p in parts],
        in_specs=[pl.BlockSpec(memory_space=pl.ANY)] * n,
        out_specs=[pl.BlockSpec(memory_space=pl.ANY)] * n,
        scratch_shapes=[pltpu.SemaphoreType.DMA((n,)), pltpu.SemaphoreType.DMA((n,))],
        name="swap_with_sibling",
    )(*parts)


def _row_tile(rows, cols, target_elems=512 * 1024):
    t = max(8, min(rows, (target_elems // cols) // 8 * 8))
    while rows % t:
        t -= 8
    return t


def _sum_slots(recv):
    _, rows, cols = recv.shape
    tr = _row_tile(rows, cols)

    def body(r_ref, o_ref):
        acc = r_ref[0].astype(F32) + r_ref[1].astype(F32)
        acc = acc + r_ref[2].astype(F32)
        o_ref[...] = acc + r_ref[3].astype(F32)

    return pl.pallas_call(
        body, grid=(rows // tr,),
        in_specs=[pl.BlockSpec((N_CHIPS, tr, cols), lambda i: (0, i, 0))],
        out_specs=pl.BlockSpec((tr, cols), lambda i: (i, 0)),
        out_shape=jax.ShapeDtypeStruct((rows, cols), F32),
        compiler_params=_cp(("parallel",)), name="sum_slots",
    )(recv)


def _adam_math(w, g, m, v):
    m = ADAM_B1 * m + (1.0 - ADAM_B1) * g
    v = ADAM_B2 * v + (1.0 - ADAM_B2) * (g * g)
    m_hat = m / (1.0 - ADAM_B1 ** ADAM_STEP)
    v_hat = v / (1.0 - ADAM_B2 ** ADAM_STEP)
    delta = -ADAM_LR * (m_hat / (jnp.sqrt(v_hat) + ADAM_EPS) + ADAM_WD * w)
    return delta, m, v


def _adam(w, g_parts, m, v):
    rows, cols = w.shape
    tr = _row_tile(rows, cols, 128 * 1024)
    ng = len(g_parts)

    def body(*refs):
        w_ref, m_ref, v_ref = refs[0], refs[1], refs[2]
        g_refs = refs[3:3 + ng]
        g_out, d_out, m_out, v_out = refs[3 + ng:]
        g = g_refs[0][...]
        for r in g_refs[1:]:
            g = g + r[...]
        delta, nm, nv = _adam_math(w_ref[...], g, m_ref[...], v_ref[...])
        g_out[...] = g
        d_out[...] = delta
        m_out[...] = nm
        v_out[...] = nv

    spec = pl.BlockSpec((tr, cols), lambda i: (i, 0))
    shp = jax.ShapeDtypeStruct((rows, cols), F32)
    return pl.pallas_call(
        body, grid=(rows // tr,),
        in_specs=[spec] * (3 + ng), out_specs=[spec] * 4, out_shape=[shp] * 4,
        compiler_params=_cp(("parallel",)), name="adam",
    )(w, m, v, *g_parts)


def _small_reduce_adam(gathered, w, m, v):
    _, rows, cols = gathered.shape

    def body(g_ref, w_ref, m_ref, v_ref, g_out, d_out, m_out, v_out):
        g = g_ref[0]
        for k in range(1, N_DEV):
            g = g + g_ref[k]
        delta, nm, nv = _adam_math(w_ref[...], g, m_ref[...], v_ref[...])
        g_out[...] = g
        d_out[...] = delta
        m_out[...] = nm
        v_out[...] = nv

    shp = jax.ShapeDtypeStruct((rows, cols), F32)
    return pl.pallas_call(body, out_shape=[shp] * 4, name="small_reduce_adam")(gathered, w, m, v)


def _modulate(x, modv, row_s, row_sh, tm):
    s_len, dm = x.shape

    def body(x_ref, mod_ref, u_ref):
        s = mod_ref[row_s:row_s + 1, :]
        sh = mod_ref[row_sh:row_sh + 1, :]
        u_ref[...] = (x_ref[...] * (1.0 + s) + sh).astype(BF16)

    return pl.pallas_call(
        body, grid=(s_len // tm,),
        in_specs=[pl.BlockSpec((tm, dm), lambda i: (i, 0)), pl.BlockSpec((8, dm), lambda i: (0, 0))],
        out_specs=pl.BlockSpec((tm, dm), lambda i: (i, 0)),
        out_shape=jax.ShapeDtypeStruct((s_len, dm), BF16),
        compiler_params=_cp(("parallel",)), name="modulate",
    )(x, modv)


def _loss_head(y, target, tm):
    s_len, dm = y.shape

    def body(y_ref, t_ref, dy_ref, l_ref):
        i = pl.program_id(0)

        @pl.when(i == 0)
        def _():
            l_ref[...] = jnp.zeros_like(l_ref)

        diff = y_ref[...] - t_ref[...]
        dy_ref[...] = diff / dm
        per_tok = jnp.mean(diff * diff, axis=-1, keepdims=True)
        l_ref[...] += 0.5 * jnp.sum(per_tok, axis=0, keepdims=True)

    return pl.pallas_call(
        body, grid=(s_len // tm,),
        in_specs=[pl.BlockSpec((tm, dm), lambda i: (i, 0))] * 2,
        out_specs=[pl.BlockSpec((tm, dm), lambda i: (i, 0)), pl.BlockSpec((8, 128), lambda i: (0, 0))],
        out_shape=[jax.ShapeDtypeStruct((s_len, dm), F32), jax.ShapeDtypeStruct((8, 128), F32)],
        compiler_params=_cp(("arbitrary",)), name="loss_head",
    )(y, target)


def _ada_forward(c_all, w_ada, b_cols):
    nl, dm, n = w_ada.shape

    def body(c_ref, w_ref, b_ref, o_ref):
        cv = c_ref[...]
        sc = (cv * _sigmoid(cv)).astype(BF16)
        o_ref[...] = _nn(sc, w_ref[...].astype(BF16)) + b_ref[...]

    return pl.pallas_call(
        body, grid=(nl,),
        in_specs=[pl.BlockSpec((N_DEV, dm), lambda l: (0, 0)),
                  pl.BlockSpec((None, dm, n), lambda l: (l, 0, 0)),
                  pl.BlockSpec((None, N_DEV, n), lambda l: (l, 0, 0))],
        out_specs=pl.BlockSpec((None, N_DEV, n), lambda l: (l, 0, 0)),
        out_shape=jax.ShapeDtypeStruct((nl, N_DEV, n), F32),
        compiler_params=_cp(("parallel",)), name="ada_forward",
    )(c_all, w_ada, b_cols)


def _ada_backward(c_pad, dmod_pad):
    nl, npad, n = dmod_pad.shape
    dm = c_pad.shape[1]

    def body(c_ref, d_ref, o_ref):
        cv = c_ref[...]
        sc = (cv * _sigmoid(cv)).astype(BF16)
        o_ref[...] = _tn(sc, d_ref[...].astype(BF16))

    return pl.pallas_call(
        body, grid=(nl,),
        in_specs=[pl.BlockSpec((npad, dm), lambda l: (0, 0)),
                  pl.BlockSpec((None, npad, n), lambda l: (l, 0, 0))],
        out_specs=pl.BlockSpec((None, dm, n), lambda l: (l, 0, 0)),
        out_shape=jax.ShapeDtypeStruct((nl, dm, n), F32),
        compiler_params=_cp(("parallel",)), name="ada_backward",
    )(c_pad, dmod_pad)


def _in_proj(u, w3, t_lo, n_t, tn, out_dtype, tm, name):
    s_len, kd = u.shape
    n = w3.shape[2]
    tps = n // tn

    def body(u_ref, w_ref, o_ref):
        o_ref[...] = _nn(u_ref[...], w_ref[...]).astype(out_dtype)

    return pl.pallas_call(
        body, grid=(s_len // tm, n_t),
        in_specs=[pl.BlockSpec((tm, kd), lambda i, t: (i, 0)),
                  pl.BlockSpec((None, kd, tn), lambda i, t: ((t + t_lo) // tps, 0, (t + t_lo) % tps))],
        out_specs=pl.BlockSpec((tm, tn), lambda i, t: (i, t)),
        out_shape=jax.ShapeDtypeStruct((s_len, n_t * tn), out_dtype),
        compiler_params=_cp(("parallel", "arbitrary")), name=name,
    )(u, w3)


def _mix_out(ya, o_g, lse_g, gates, x_in, modv, lnv, w_a, w_b, w_o, tm):
    s_len = ya.shape[0]

    def body(ya_ref, o0, o1, o2, l0, l1, l2, g_ref, x_ref, mod_ref, ln_ref, wa_ref, wb_ref, wo_ref,
             xm_ref, xhat_ref, rstd_ref, u2_ref, t1_ref, yb_ref):
        la, lb, lc = l0[...], l1[...], l2[...]
        mx = jnp.maximum(jnp.maximum(la, lb), lc)
        ea, eb, ec = jnp.exp(la - mx), jnp.exp(lb - mx), jnp.exp(lc - mx)
        den = ea + eb + ec
        yb = (o0[...] * (ea / den) + o1[...] * (eb / den) + o2[...] * (ec / den)).astype(BF16)
        yb_ref[...] = yb
        yav = ya_ref[...]
        za = jnp.concatenate([_nn(yav, wa_ref[j]) for j in range(N_CHIPS)], axis=1)
        zb = jnp.concatenate([_nn(yb, wb_ref[j]) for j in range(N_CHIPS)], axis=1)
        merged = _sigmoid(g_ref[:, 0:D]) * za + _sigmoid(g_ref[:, D:2 * D]) * zb
        t1 = _nn(merged.astype(BF16), wo_ref[...])
        t1_ref[...] = t1.astype(BF16)
        r = ALPHA * x_ref[...] + mod_ref[2:3, :] * t1
        xhat, rstd, xm = _layer_norm_rows(r, ln_ref[0:1, :], ln_ref[1:2, :])
        xhat_ref[...] = xhat
        rstd_ref[...] = jnp.broadcast_to(rstd, (tm, 128))
        xm_ref[...] = xm
        u2_ref[...] = (xm * (1.0 + mod_ref[4:5, :]) + mod_ref[3:4, :]).astype(BF16)

    tok = lambda w: pl.BlockSpec((tm, w), lambda i: (i, 0))
    full = lambda shp: pl.BlockSpec(shp, lambda i: (0,) * len(shp))
    return pl.pallas_call(
        body, grid=(s_len // tm,),
        in_specs=[tok(A_QW)] + [tok(B_OW)] * 6 + [tok(2 * D), tok(D), full((8, D)), full((8, D)),
                  full(w_a.shape), full(w_b.shape), full(w_o.shape)],
        out_specs=[tok(D), tok(D), tok(128), tok(D), tok(D), tok(B_OW)],
        out_shape=[jax.ShapeDtypeStruct((s_len, D), F32), jax.ShapeDtypeStruct((s_len, D), F32),
                   jax.ShapeDtypeStruct((s_len, 128), F32), jax.ShapeDtypeStruct((s_len, D), BF16),
                   jax.ShapeDtypeStruct((s_len, D), BF16), jax.ShapeDtypeStruct((s_len, B_OW), BF16)],
        compiler_params=_cp(("parallel",)), name="mix_out",
    )(ya, *o_g, *lse_g, gates, x_in, modv, lnv, w_a, w_b, w_o)


def _ffn_forward(u2, x_mid, modv, lnv, mod_next, w_g, w_u, w_d, tm):
    s_len = u2.shape[0]

    def body(u_ref, xm_ref, mod_ref, ln_ref, nxt_ref, wg_ref, wu_ref, wd_ref,
             a_ref, b_ref, t2_ref, xhat_ref, rstd_ref, xo_ref, un_ref, acc_ref):
        j = pl.program_id(1)

        @pl.when(j == 0)
        def _():
            acc_ref[...] = jnp.zeros_like(acc_ref)

        uv = u_ref[...]
        a = _nn(uv, wg_ref[...])
        b = _nn(uv, wu_ref[...])
        a_ref[...] = a
        b_ref[...] = b
        f = (a * _sigmoid(a)) * b
        acc_ref[...] += _nn(f.astype(BF16), wd_ref[...])

        @pl.when(j == N_CHIPS - 1)
        def _():
            t2 = acc_ref[...]
            t2_ref[...] = t2.astype(BF16)
            r = ALPHA * xm_ref[...] + mod_ref[5:6, :] * t2
            xhat, rstd, xo = _layer_norm_rows(r, ln_ref[2:3, :], ln_ref[3:4, :])
            xhat_ref[...] = xhat
            rstd_ref[...] = jnp.broadcast_to(rstd, (tm, 128))
            xo_ref[...] = xo
            un_ref[...] = (xo * (1.0 + nxt_ref[1:2, :]) + nxt_ref[0:1, :]).astype(BF16)

    tok = lambda w: pl.BlockSpec((tm, w), lambda i, j: (i, 0))
    full = lambda shp: pl.BlockSpec(shp, lambda i, j: (0,) * len(shp))
    hid = pl.BlockSpec((None, tm, FF_S), lambda i, j: (j, i, 0))
    return pl.pallas_call(
        body, grid=(s_len // tm, N_CHIPS),
        in_specs=[tok(D), tok(D), full((8, D)), full((8, D)), full((8, D)),
                  pl.BlockSpec((None, D, FF_S), lambda i, j: (j, 0, 0)),
                  pl.BlockSpec((None, D, FF_S), lambda i, j: (j, 0, 0)),
                  pl.BlockSpec((None, FF_S, D), lambda i, j: (j, 0, 0))],
        out_specs=[hid, hid, tok(D), tok(D), tok(128), tok(D), tok(D)],
        out_shape=[jax.ShapeDtypeStruct((N_CHIPS, s_len, FF_S), F32),
                   jax.ShapeDtypeStruct((N_CHIPS, s_len, FF_S), F32),
                   jax.ShapeDtypeStruct((s_len, D), BF16), jax.ShapeDtypeStruct((s_len, D), F32),
                   jax.ShapeDtypeStruct((s_len, 128), F32), jax.ShapeDtypeStruct((s_len, D), F32),
                   jax.ShapeDtypeStruct((s_len, D), BF16)],
        scratch_shapes=[pltpu.VMEM((tm, D), F32)],
        compiler_params=_cp(("parallel", "arbitrary")), name="ffn_forward",
    )(u2, x_mid, modv, lnv, mod_next, w_g, w_u, w_d)


def _ffn_backward(dxo, xhat2, rstd2, t2, x_mid, a, b, modv, lnv, w_g, w_u, w_d, tm):
    s_len = dxo.shape[0]

    def body(dxo_ref, xhat_ref, rstd_ref, t2_ref, xm_ref, a_ref, b_ref, mod_ref, ln_ref, wg_ref, wu_ref, wd_ref,
             dxm_ref, dt2_ref, da_ref, db_ref, f_ref, small_ref, dr_s, dt_s, acc_ref):
        i, j = pl.program_id(0), pl.program_id(1)

        @pl.when((i == 0) & (j == 0))
        def _():
            small_ref[...] = jnp.zeros_like(small_ref)

        @pl.when(j == 0)
        def _():
            dxov, xhat = dxo_ref[...], xhat_ref[...]
            dr = _layer_norm_bwd_rows(dxov, xhat, rstd_ref[:, 0:1], ln_ref[2:3, :])
            small_ref[0:1, :] += _colsum(dxov * xhat)
            small_ref[1:2, :] += _colsum(dxov)
            small_ref[2:3, :] += _colsum(dr * t2_ref[...].astype(F32))
            dt = (mod_ref[5:6, :] * dr).astype(BF16)
            dt2_ref[...] = dt
            dt_s[...] = dt
            dr_s[...] = dr
            acc_ref[...] = jnp.zeros_like(acc_ref)

        df = _nt(dt_s[...], wd_ref[...])
        av, bv = a_ref[...], b_ref[...]
        sa = _sigmoid(av)
        sl = av * sa
        da = (df * bv * (sa * (1.0 + av * (1.0 - sa)))).astype(BF16)
        db = (df * sl).astype(BF16)
        da_ref[...] = da
        db_ref[...] = db
        f_ref[...] = (sl * bv).astype(BF16)
        acc_ref[...] += _nt(da, wg_ref[...]) + _nt(db, wu_ref[...])

        @pl.when(j == N_CHIPS - 1)
        def _():
            du = acc_ref[...]
            dxm_ref[...] = ALPHA * dr_s[...] + du * (1.0 + mod_ref[4:5, :])
            small_ref[3:4, :] += _colsum(du * xm_ref[...])
            small_ref[4:5, :] += _colsum(du)

    tok = lambda w: pl.BlockSpec((tm, w), lambda i, j: (i, 0))
    full = lambda shp: pl.BlockSpec(shp, lambda i, j: (0,) * len(shp))
    hid = pl.BlockSpec((None, tm, FF_S), lambda i, j: (j, i, 0))
    hid_bf = jax.ShapeDtypeStruct((N_CHIPS, s_len, FF_S), BF16)
    return pl.pallas_call(
        body, grid=(s_len // tm, N_CHIPS),
        in_specs=[tok(D), tok(D), tok(128), tok(D), tok(D), hid, hid, full((8, D)), full((8, D)),
                  pl.BlockSpec((None, D, FF_S), lambda i, j: (j, 0, 0)),
                  pl.BlockSpec((None, D, FF_S), lambda i, j: (j, 0, 0)),
                  pl.BlockSpec((None, FF_S, D), lambda i, j: (j, 0, 0))],
        out_specs=[tok(D), tok(D), hid, hid, hid, full((8, D))],
        out_shape=[jax.ShapeDtypeStruct((s_len, D), F32), jax.ShapeDtypeStruct((s_len, D), BF16),
                   hid_bf, hid_bf, hid_bf, jax.ShapeDtypeStruct((8, D), F32)],
        scratch_shapes=[pltpu.VMEM((tm, D), F32), pltpu.VMEM((tm, D), BF16), pltpu.VMEM((tm, D), F32)],
        compiler_params=_cp(("arbitrary", "arbitrary")), name="ffn_backward",
    )(dxo, xhat2, rstd2, t2, x_mid, a, b, modv, lnv, w_g, w_u, w_d)


def _mix_backward(dxm, xhat1, rstd1, t1, gates, ya, yb, modv, lnv, w_a, w_b, w_o, tm):
    s_len = dxm.shape[0]

    def body(dxm_ref, xhat_ref, rstd_ref, t1_ref, g_ref, ya_ref, yb_ref, mod_ref, ln_ref, wa_ref, wb_ref, wo_ref,
             dxi_ref, dt1_ref, mg_ref, dza_ref, dzb_ref, dg_ref, dya_ref, dyb_ref, small_ref):
        i = pl.program_id(0)

        @pl.when(i == 0)
        def _():
            small_ref[...] = jnp.zeros_like(small_ref)

        dxmv, xhat = dxm_ref[...], xhat_ref[...]
        dr = _layer_norm_bwd_rows(dxmv, xhat, rstd_ref[:, 0:1], ln_ref[0:1, :])
        small_ref[0:1, :] += _colsum(dxmv * xhat)
        small_ref[1:2, :] += _colsum(dxmv)
        small_ref[2:3, :] += _colsum(dr * t1_ref[...].astype(F32))
        dxi_ref[...] = ALPHA * dr
        dt1 = (mod_ref[2:3, :] * dr).astype(BF16)
        dt1_ref[...] = dt1
        dmg = _nt(dt1, wo_ref[...])
        yav, ybv = ya_ref[...], yb_ref[...]
        za = jnp.concatenate([_nn(yav, wa_ref[j]) for j in range(N_CHIPS)], axis=1)
        zb = jnp.concatenate([_nn(ybv, wb_ref[j]) for j in range(N_CHIPS)], axis=1)
        sga, sgb = _sigmoid(g_ref[:, 0:D]), _sigmoid(g_ref[:, D:2 * D])
        mg_ref[...] = (sga * za + sgb * zb).astype(BF16)
        dza = (dmg * sga).astype(BF16)
        dzb = (dmg * sgb).astype(BF16)
        dza_ref[...] = dza
        dzb_ref[...] = dzb
        dg_ref[:, 0:D] = (dmg * za * (sga * (1.0 - sga))).astype(BF16)
        dg_ref[:, D:2 * D] = (dmg * zb * (sgb * (1.0 - sgb))).astype(BF16)
        cw = D // N_CHIPS
        dya = _nt(dza[:, 0:cw], wa_ref[0])
        dyb = _nt(dzb[:, 0:cw], wb_ref[0])
        for j in range(1, N_CHIPS):
            dya = dya + _nt(dza[:, j * cw:(j + 1) * cw], wa_ref[j])
            dyb = dyb + _nt(dzb[:, j * cw:(j + 1) * cw], wb_ref[j])
        dya_ref[...] = dya.astype(BF16)
        dyb_ref[...] = dyb

    tok = lambda w: pl.BlockSpec((tm, w), lambda i: (i, 0))
    full = lambda shp: pl.BlockSpec(shp, lambda i: (0,) * len(shp))
    sd = lambda w, dt: jax.ShapeDtypeStruct((s_len, w), dt)
    return pl.pallas_call(
        body, grid=(s_len // tm,),
        in_specs=[tok(D), tok(D), tok(128), tok(D), tok(2 * D), tok(A_QW), tok(B_OW), full((8, D)), full((8, D)),
                  full(w_a.shape), full(w_b.shape), full(w_o.shape)],
        out_specs=[tok(D), tok(D), tok(D), tok(D), tok(D), tok(2 * D), tok(A_QW), tok(B_OW), full((8, D))],
        out_shape=[sd(D, F32), sd(D, BF16), sd(D, BF16), sd(D, BF16), sd(D, BF16), sd(2 * D, BF16),
                   sd(A_QW, BF16), sd(B_OW, F32), jax.ShapeDtypeStruct((8, D), F32)],
        compiler_params=_cp(("arbitrary",)), name="mix_backward",
    )(dxm, xhat1, rstd1, t1, gates, ya, yb, modv, lnv, w_a, w_b, w_o)


def _split3(v):
    hi = v.astype(BF16)
    r1 = v - hi.astype(F32)
    mid = r1.astype(BF16)
    lo = (r1 - mid.astype(F32)).astype(BF16)
    return hi, mid, lo


def _group_mix_backward(dyb, o_g, lse_g, tm):
    s_len = dyb.shape[0]

    def body(dyb_ref, o0, o1, o2, l0, l1, l2, do0, do1, do2, dl0, dl1, dl2):
        rr = lax.shift_right_logical(lax.broadcasted_iota(jnp.int32, (B_OW, B_OW), 0), 6)
        cc = lax.shift_right_logical(lax.broadcasted_iota(jnp.int32, (B_OW, B_OW), 1), 6)
        ones_bd = jnp.where(rr == cc, 1.0, 0.0).astype(BF16)

        def head_sum(v):
            hi, mid, lo = _split3(v)
            return _nn(hi, ones_bd) + _nn(mid, ones_bd) + _nn(lo, ones_bd)

        la, lb, lc = l0[...], l1[...], l2[...]
        mx = jnp.maximum(jnp.maximum(la, lb), lc)
        ea, eb, ec = jnp.exp(la - mx), jnp.exp(lb - mx), jnp.exp(lc - mx)
        den = ea + eb + ec
        wts = (ea / den, eb / den, ec / den)
        dy = dyb_ref[...]
        dws = [head_sum(dy * o[...]) for o in (o0, o1, o2)]
        dot = wts[0] * dws[0] + wts[1] * dws[1] + wts[2] * dws[2]
        for wt, dw, do_ref, dl_ref in zip(wts, dws, (do0, do1, do2), (dl0, dl1, dl2)):
            do_ref[...] = (dy * wt).astype(BF16)
            dl_ref[...] = wt * (dw - dot)

    tok = pl.BlockSpec((tm, B_OW), lambda i: (i, 0))
    return pl.pallas_call(
        body, grid=(s_len // tm,),
        in_specs=[tok] * 7, out_specs=[tok] * 6,
        out_shape=[jax.ShapeDtypeStruct((s_len, B_OW), BF16)] * 3 + [jax.ShapeDtypeStruct((s_len, B_OW), F32)] * 3,
        compiler_params=_cp(("parallel",)), name="group_mix_backward",
    )(dyb, *o_g, *lse_g)


def _in_proj_backward(dh, w_in, dxi, x_in, modv, tm):
    s_len = dh.shape[0]

    def body(dh_ref, w_ref, dxi_ref, x_ref, mod_ref, dx_ref, small_ref, acc_ref):
        i, j = pl.program_id(0), pl.program_id(1)

        @pl.when((i == 0) & (j == 0))
        def _():
            small_ref[...] = jnp.zeros_like(small_ref)

        @pl.when(j == 0)
        def _():
            acc_ref[...] = jnp.zeros_like(acc_ref)

        acc_ref[...] += _nt(dh_ref[...], w_ref[...])

        @pl.when(j == N_CHIPS - 1)
        def _():
            du = acc_ref[...]
            dx_ref[...] = dxi_ref[...] + du * (1.0 + mod_ref[1:2, :])
            small_ref[0:1, :] += _colsum(du * x_ref[...])
            small_ref[1:2, :] += _colsum(du)

    tok = lambda w: pl.BlockSpec((tm, w), lambda i, j: (i, 0))
    return pl.pallas_call(
        body, grid=(s_len // tm, N_CHIPS),
        in_specs=[pl.BlockSpec((tm, IN_S), lambda i, j: (i, j)),
                  pl.BlockSpec((None, D, IN_S), lambda i, j: (j, 0, 0)),
                  tok(D), tok(D), pl.BlockSpec((8, D), lambda i, j: (0, 0))],
        out_specs=[tok(D), pl.BlockSpec((8, D), lambda i, j: (0, 0))],
        out_shape=[jax.ShapeDtypeStruct((s_len, D), F32), jax.ShapeDtypeStruct((8, D), F32)],
        scratch_shapes=[pltpu.VMEM((tm, D), F32)],
        compiler_params=_cp(("arbitrary", "arbitrary")), name="in_proj_backward",
    )(dh, w_in, dxi, x_in, modv)


def _weight_grad(a, b, *, a_block, a_map, b_block, b_map, out_shape, out_block, out_map, n_panels, s_len, ts, name):
    acc_shape = tuple(d for d in out_block if d is not None)

    def body(a_ref, b_ref, o_ref, acc_ref):
        s = pl.program_id(1)

        @pl.when(s == 0)
        def _():
            acc_ref[...] = jnp.zeros_like(acc_ref)

        acc_ref[...] += _tn(a_ref[...], b_ref[...])

        @pl.when(s == s_len // ts - 1)
        def _():
            o_ref[...] = acc_ref[...].astype(BF16)

    return pl.pallas_call(
        body, grid=(n_panels, s_len // ts),
        in_specs=[pl.BlockSpec(a_block, a_map), pl.BlockSpec(b_block, b_map)],
        out_specs=pl.BlockSpec(out_block, out_map),
        out_shape=jax.ShapeDtypeStruct(out_shape, BF16),
        scratch_shapes=[pltpu.VMEM(acc_shape, F32)],
        compiler_params=_cp(("parallel", "arbitrary")), name=name,
    )(a, b)


def _bias_tables(slopes, max_dist, stride):
    qi = np.arange(BLK)[:, None]
    sj = np.arange(2 * BLK)[None, :]
    dist = qi + BLK - sj
    valid = (dist >= 0) & (dist <= max_dist)
    bias = -(jnp.asarray(slopes, F32).reshape(-1, 1, 1) * jnp.asarray(dist * stride, F32))
    gen = jnp.where(valid[None], bias, NEG_INF)
    first = jnp.where((valid & (sj >= BLK))[None], bias, NEG_INF)
    return jnp.stack([gen, first]).astype(F32)


def _attn_specs(dil, n_heads, n_kv, q_off, k_off, v_off, nb, total_w):
    qw, kw = n_heads * HD, n_kv * HD
    qs, ks = total_w // qw, total_w // kw
    q_spec = pl.BlockSpec((BLK, qw), lambda r, n: (jnp.minimum(n, nb - 1), r * qs + q_off // qw))
    kp = pl.BlockSpec((BLK, kw), lambda r, n: (jnp.maximum(n - 1, 0), r * ks + k_off // kw))
    kc = pl.BlockSpec((BLK, kw), lambda r, n: (jnp.minimum(n, nb - 1), r * ks + k_off // kw))
    vp = pl.BlockSpec((BLK, kw), lambda r, n: (jnp.maximum(n - 1, 0), r * ks + v_off // kw))
    vc = pl.BlockSpec((BLK, kw), lambda r, n: (jnp.minimum(n, nb - 1), r * ks + v_off // kw))
    return q_spec, kp, kc, vp, vc


def _attn_forward(qkv_f, bias, sinks, *, dil, n_heads, n_kv, q_off, k_off, v_off, out_dtype, name):
    seq, tw = qkv_f.shape
    total_w = tw // dil
    nb = seq // BLK
    grp = n_heads // n_kv
    qw = n_heads * HD
    has_sink = sinks is not None

    def body(*refs):
        if has_sink:
            sink_ref, refs = refs[0], refs[1:]
        q_ref, kp_ref, kc_ref, vp_ref, vc_ref, bias_ref, o_ref, lse_ref = refs
        n = pl.program_id(1)
        sel = jnp.where(n == 0, 1, 0)
        for h in range(n_heads):
            hk = h // grp
            hs, ks = slice(h * HD, (h + 1) * HD), slice(hk * HD, (hk + 1) * HD)
            qh = q_ref[:, hs] * 0.125
            sp = _nt(qh, kp_ref[:, ks]) + bias_ref[sel, h, :, 0:BLK]
            sc = _nt(qh, kc_ref[:, ks]) + bias_ref[sel, h, :, BLK:2 * BLK]
            m = jnp.maximum(jnp.max(sp, axis=-1, keepdims=True), jnp.max(sc, axis=-1, keepdims=True))
            if has_sink:
                m = jnp.maximum(m, sink_ref[0, h])
            ep, ec = jnp.exp(sp - m), jnp.exp(sc - m)
            den = jnp.sum(ep, axis=-1, keepdims=True) + jnp.sum(ec, axis=-1, keepdims=True)
            if has_sink:
                den = den + jnp.exp(sink_ref[0, h] - m)
            pv = _nn(ep.astype(BF16), vp_ref[:, ks]) + _nn(ec.astype(BF16), vc_ref[:, ks])
            o_ref[:, hs] = (pv * (1.0 / den)).astype(out_dtype)
            lse_ref[:, hs] = jnp.broadcast_to(m + jnp.log(den), (BLK, HD))

    q_spec, kp, kc, vp, vc = _attn_specs(dil, n_heads, n_kv, q_off, k_off, v_off, nb, total_w)
    o_spec = pl.BlockSpec((BLK, qw), lambda r, n: (n, r))
    in_specs = [q_spec, kp, kc, vp, vc, pl.BlockSpec(bias.shape, lambda r, n: (0, 0, 0, 0))]
    args = [qkv_f, qkv_f, qkv_f, qkv_f, qkv_f, bias]
    if has_sink:
        in_specs = [pl.BlockSpec(memory_space=pltpu.SMEM)] + in_specs
        args = [sinks] + args
    return pl.pallas_call(
        body, grid=(dil, nb), in_specs=in_specs, out_specs=[o_spec, o_spec],
        out_shape=[jax.ShapeDtypeStruct((seq, dil * qw), out_dtype), jax.ShapeDtypeStruct((seq, dil * qw), F32)],
        compiler_params=_cp(("parallel", "arbitrary")), name=name,
    )(*args)


def _attn_backward(qkv_f, do_f, lse_f, dlse_f, bias, sinks, *, dil, n_heads, n_kv, q_off, k_off, v_off, name):
    seq, tw = qkv_f.shape
    total_w = tw // dil
    nb = seq // BLK
    grp = n_heads // n_kv
    qw, kw = n_heads * HD, n_kv * HD
    has_sink = sinks is not None
    has_dlse = dlse_f is not None

    def body(*refs):
        refs = list(refs)
        sink_ref = refs.pop(0) if has_sink else None
        q_ref, kp_ref, kc_ref, vp_ref, vc_ref, do_ref, lse_ref = refs[:7]
        refs = refs[7:]
        dlse_ref = refs.pop(0) if has_dlse else None
        bias_ref, dq_ref, dk_ref, dv_ref, ds_ref, ck_ref, cv_ref = refs
        r, n = pl.program_id(0), pl.program_id(1)

        @pl.when((r == 0) & (n == 0))
        def _():
            ds_ref[...] = jnp.zeros_like(ds_ref)

        @pl.when(n == 0)
        def _():
            ck_ref[...] = jnp.zeros_like(ck_ref)
            cv_ref[...] = jnp.zeros_like(cv_ref)

        @pl.when(n < nb)
        def _():
            sel = jnp.where(n == 0, 1, 0)
            for hk in range(n_kv):
                ks = slice(hk * HD, (hk + 1) * HD)
                kpv, kcv, vpv, vcv = kp_ref[:, ks], kc_ref[:, ks], vp_ref[:, ks], vc_ref[:, ks]
                dkp = jnp.zeros((BLK, HD), F32)
                dkc = jnp.zeros((BLK, HD), F32)
                dvp = jnp.zeros((BLK, HD), F32)
                dvc = jnp.zeros((BLK, HD), F32)
                for h in range(hk * grp, (hk + 1) * grp):
                    hs = slice(h * HD, (h + 1) * HD)
                    qh = q_ref[:, hs] * 0.125
                    doh = do_ref[:, hs]
                    lse = lse_ref[:, h * HD:h * HD + 1]
                    pp = jnp.exp(_nt(qh, kpv) + bias_ref[sel, h, :, 0:BLK] - lse)
                    pc = jnp.exp(_nt(qh, kcv) + bias_ref[sel, h, :, BLK:2 * BLK] - lse)
                    dpp, dpc = _nt(doh, vpv), _nt(doh, vcv)
                    delta = (jnp.sum(pp * dpp, axis=-1, keepdims=True)
                             + jnp.sum(pc * dpc, axis=-1, keepdims=True))
                    shift = delta
                    if has_dlse:
                        shift = delta - dlse_ref[:, h * HD:h * HD + 1]
                    dsp = (pp * (dpp - shift)).astype(BF16)
                    dsc = (pc * (dpc - shift)).astype(BF16)
                    dq_ref[:, hs] = ((_nn(dsp, kpv) + _nn(dsc, kcv)) * 0.125).astype(BF16)
                    dkp = dkp + _tn(dsp, qh)
                    dkc = dkc + _tn(dsc, qh)
                    dvp = dvp + _tn(pp.astype(BF16), doh)
                    dvc = dvc + _tn(pc.astype(BF16), doh)
                    if has_sink:
                        psink = jnp.exp(sink_ref[0, h] - lse)
                        ds_ref[h:h + 1, :] += jnp.broadcast_to(-jnp.sum(psink * delta, axis=0, keepdims=True), (1, 128))
                dk_ref[:, ks] = (ck_ref[:, ks] + dkp).astype(BF16)
                dv_ref[:, ks] = (cv_ref[:, ks] + dvp).astype(BF16)
                ck_ref[:, ks] = dkc
                cv_ref[:, ks] = dvc

        @pl.when(n == nb)
        def _():
            dk_ref[...] = ck_ref[...].astype(BF16)
            dv_ref[...] = cv_ref[...].astype(BF16)

    q_spec, kp, kc, vp, vc = _attn_specs(dil, n_heads, n_kv, q_off, k_off, v_off, nb, total_w)
    qo_spec = pl.BlockSpec((BLK, qw), lambda r, n: (jnp.minimum(n, nb - 1), r))
    ko_spec = pl.BlockSpec((BLK, kw), lambda r, n: (jnp.maximum(n - 1, 0), r))
    in_specs = [q_spec, kp, kc, vp, vc, qo_spec, qo_spec]
    args = [qkv_f, qkv_f, qkv_f, qkv_f, qkv_f, do_f, lse_f]
    if has_dlse:
        in_specs.append(qo_spec)
        args.append(dlse_f)
    in_specs.append(pl.BlockSpec(bias.shape, lambda r, n: (0, 0, 0, 0)))
    args.append(bias)
    if has_sink:
        in_specs = [pl.BlockSpec(memory_space=pltpu.SMEM)] + in_specs
        args = [sinks] + args
    return pl.pallas_call(
        body, grid=(dil, nb + 1), in_specs=in_specs,
        out_specs=[qo_spec, ko_spec, ko_spec, pl.BlockSpec((8, 128), lambda r, n: (0, 0))],
        out_shape=[jax.ShapeDtypeStruct((seq, dil * qw), BF16), jax.ShapeDtypeStruct((seq, dil * kw), BF16),
                   jax.ShapeDtypeStruct((seq, dil * kw), BF16), jax.ShapeDtypeStruct((8, 128), F32)],
        scratch_shapes=[pltpu.VMEM((BLK, kw), F32), pltpu.VMEM((BLK, kw), F32)],
        compiler_params=_cp(("arbitrary", "arbitrary")), name=name,
    )(*args)


def _alibi_slopes():
    return jnp.exp2(-8.0 * jnp.arange(1, N_HEADS + 1, dtype=F32) / N_HEADS)


def _fold(v, dil):
    s_len, w = v.shape
    return v.reshape(s_len // dil, dil * w)


def _unfold(v, dil):
    seq, w = v.shape
    return v.reshape(seq * dil, w // dil)


def _b_offsets(g):
    q0 = A_QW + 2 * A_KW
    return q0 + g * B_OW, q0 + B_W + g * B_OW, q0 + 2 * B_W + g * B_OW


def _layer_forward(x_in, u1, modv, lnv, mod_next, sinks_l, wts, tabs, tm):
    w_in, w_a, w_b, w_o, w_g, w_u, w_d = wts
    qkv = _in_proj(u1, w_in, 0, QKV_W // 256, 256, BF16, tm, "in_proj_qkv")
    gates = _in_proj(u1, w_in, QKV_W // 256, 2 * D // 256, 256, F32, tm, "in_proj_gates")
    ya, lse_a = _attn_forward(qkv, tabs[0], sinks_l, dil=1, n_heads=A_H, n_kv=A_HKV, q_off=0, k_off=A_QW,
                              v_off=A_QW + A_KW, out_dtype=BF16, name="attn_a_forward")
    o_g, lse_g = [], []
    for g, (_, dil) in enumerate(B_GROUPS):
        qo, ko, vo = _b_offsets(g)
        o, l = _attn_forward(_fold(qkv, dil), tabs[1 + g], None, dil=dil, n_heads=B_HG, n_kv=B_HG,
                             q_off=qo, k_off=ko, v_off=vo, out_dtype=F32, name="attn_b%d_forward" % g)
        o_g.append(_unfold(o, dil))
        lse_g.append(_unfold(l, dil))
    x_mid, xhat1, rstd1, u2, t1, yb = _mix_out(ya, o_g, lse_g, gates, x_in, modv, lnv, w_a, w_b, w_o, tm)
    a, b, t2, xhat2, rstd2, x_out, u_next = _ffn_forward(u2, x_mid, modv, lnv, mod_next, w_g, w_u, w_d, tm)
    saved = dict(x_in=x_in, u1=u1, qkv=qkv, gates=gates, ya=ya, lse_a=lse_a, o_g=o_g, lse_g=lse_g, yb=yb,
                 x_mid=x_mid, xhat1=xhat1, rstd1=rstd1, u2=u2, t1=t1, a=a, b=b, t2=t2, xhat2=xhat2, rstd2=rstd2)
    return x_out, u_next, saved


def _layer_backward(dxo, sv, modv, lnv, sinks_l, wts, tabs, tm):
    w_in, w_a, w_b, w_o, w_g, w_u, w_d = wts
    s_len = dxo.shape[0]
    ts = 512 if s_len % 512 == 0 else s_len
    dxm, dt2, da, db, f, small2 = _ffn_backward(dxo, sv["xhat2"], sv["rstd2"], sv["t2"], sv["x_mid"], sv["a"],
                                                sv["b"], modv, lnv, w_g, w_u, w_d, tm)
    hid_a = dict(a_block=(ts, D), a_map=lambda p, s: (s, 0), b_block=(None, ts, FF_S), b_map=lambda p, s: (p, s, 0),
                 out_shape=(N_CHIPS, D, FF_S), out_block=(None, D, FF_S), out_map=lambda p, s: (p, 0, 0),
                 n_panels=N_CHIPS, s_len=s_len, ts=ts)
    dw_g = _weight_grad(sv["u2"], da, name="dw_gate", **hid_a)
    dw_u = _weight_grad(sv["u2"], db, name="dw_up", **hid_a)
    dw_d = _weight_grad(f, dt2, a_block=(None, ts, FF_S), a_map=lambda p, s: (p, s, 0), b_block=(ts, D),
                        b_map=lambda p, s: (s, 0), out_shape=(N_CHIPS, FF_S, D), out_block=(None, FF_S, D),
                        out_map=lambda p, s: (p, 0, 0), n_panels=N_CHIPS, s_len=s_len, ts=ts, name="dw_down")
    dxi, dt1, merged, dza, dzb, dgates, dya, dyb, small1 = _mix_backward(
        dxm, sv["xhat1"], sv["rstd1"], sv["t1"], sv["gates"], sv["ya"], sv["yb"], modv, lnv, w_a, w_b, w_o, tm)
    rw = D // N_CHIPS
    dw_o = _weight_grad(merged, dt1, a_block=(ts, rw), a_map=lambda p, s: (s, p), b_block=(ts, D),
                        b_map=lambda p, s: (s, 0), out_shape=(N_CHIPS, rw, D), out_block=(None, rw, D),
                        out_map=lambda p, s: (p, 0, 0), n_panels=N_CHIPS, s_len=s_len, ts=ts, name="dw_o")
    col = dict(b_block=(ts, rw), b_map=lambda p, s: (s, p), n_panels=N_CHIPS, s_len=s_len, ts=ts)
    dw_a = _weight_grad(sv["ya"], dza, a_block=(ts, A_QW), a_map=lambda p, s: (s, 0), out_shape=(N_CHIPS, A_QW, rw),
                        out_block=(None, A_QW, rw), out_map=lambda p, s: (p, 0, 0), name="dw_a", **col)
    dw_b = _weight_grad(sv["yb"], dzb, a_block=(ts, B_OW), a_map=lambda p, s: (s, 0), out_shape=(N_CHIPS, B_OW, rw),
                        out_block=(None, B_OW, rw), out_map=lambda p, s: (p, 0, 0), name="dw_b", **col)
    qkv = sv["qkv"]
    dqa, dka, dva, dsink = _attn_backward(qkv, dya, sv["lse_a"], None, tabs[0], sinks_l, dil=1, n_heads=A_H,
                                          n_kv=A_HKV, q_off=0, k_off=A_QW, v_off=A_QW + A_KW, name="attn_a_backward")
    do_g = _group_mix_backward(dyb, sv["o_g"], sv["lse_g"], tm)
    dq_b, dk_b, dv_b = [], [], []
    for g, (_, dil) in enumerate(B_GROUPS):
        qo, ko, vo = _b_offsets(g)
        dq, dk, dv, _ = _attn_backward(_fold(qkv, dil), _fold(do_g[g], dil), _fold(sv["lse_g"][g], dil),
                                       _fold(do_g[3 + g], dil), tabs[1 + g], None, dil=dil, n_heads=B_HG, n_kv=B_HG,
                                       q_off=qo, k_off=ko, v_off=vo, name="attn_b%d_backward" % g)
        dq_b.append(_unfold(dq, dil))
        dk_b.append(_unfold(dk, dil))
        dv_b.append(_unfold(dv, dil))
    dh = jnp.concatenate([dqa, dka, dva] + dq_b + dk_b + dv_b + [dgates], axis=1)
    dw_in = _weight_grad(sv["u1"], dh, a_block=(ts, D), a_map=lambda p, s: (s, 0), b_block=(ts, IN_S),
                         b_map=lambda p, s: (s, p), out_shape=(N_CHIPS, D, IN_S), out_block=(None, D, IN_S),
                         out_map=lambda p, s: (p, 0, 0), n_panels=N_CHIPS, s_len=s_len, ts=ts, name="dw_in")
    dx_in, small0 = _in_proj_backward(dh, w_in, dxi, sv["x_in"], modv, tm)
    dmod = jnp.stack([small0[1], small0[0], small1[2], small2[4], small2[3], small2[2]])
    dln = jnp.stack([small1[0], small1[1], small2[0], small2[1]])
    grads = (dw_in, dw_a, dw_b, dw_o, dw_g, dw_u, dw_d)
    return dx_in, grads, dmod, dln, dsink[:, 0]


def _local_step(x, target, mod, sinks, ln, gathered, tm):
    slopes = _alibi_slopes()
    tabs = [_bias_tables(slopes[:A_H], A_WINDOW - 1, 1)]
    for g, (window, dil) in enumerate(B_GROUPS):
        lo = A_H + g * B_HG
        tabs.append(_bias_tables(slopes[lo:lo + B_HG], window // dil, dil))
    zeros2 = jnp.zeros((2, D), F32)
    modvs = [jnp.concatenate([mod[l].reshape(6, D), zeros2]) for l in range(DEPTH)]
    lnvs = [jnp.concatenate([jnp.stack([ln[0][l], ln[1][l], ln[2][l], ln[3][l]]), jnp.zeros((4, D), F32)])
            for l in range(DEPTH)]
    modvs.append(jnp.zeros((8, D), F32))
    u = _modulate(x, modvs[0], 1, 0, tm)
    saved = []
    for l in range(DEPTH):
        x, u, sv = _layer_forward(x, u, modvs[l], lnvs[l], modvs[l + 1], sinks[l:l + 1], gathered[l], tabs, tm)
        saved.append(sv)
    dx, loss = _loss_head(x, target, tm)
    grads, dmods, dlns, dsinks = [None] * DEPTH, [None] * DEPTH, [None] * DEPTH, [None] * DEPTH
    for l in reversed(range(DEPTH)):
        dx, grads[l], dmods[l], dlns[l], dsinks[l] = _layer_backward(
            dx, saved[l], modvs[l], lnvs[l], sinks[l:l + 1], gathered[l], tabs, tm)
    return loss, dx, grads, jnp.stack(dmods), jnp.stack(dlns), jnp.stack(dsinks)


def _pack_small(b_ada_like, ln1_g, ln1_b, ln2_g, ln2_b, sinks_like, loss_row):
    sink_row = jnp.zeros((D,), F32).at[:DEPTH * A_H].set(sinks_like.reshape(-1))
    rows = [b_ada_like.reshape(DEPTH * 6, D), ln1_g, ln1_b, ln2_g, ln2_b, sink_row[None], loss_row[None],
            jnp.zeros((SMALL_ROWS - DEPTH * 10 - 2, D), F32)]
    return jnp.concatenate(rows)


def _unpack_small(p):
    n0 = DEPTH * 6
    return (p[:n0].reshape(DEPTH, 6 * D), p[n0:n0 + 4], p[n0 + 4:n0 + 8], p[n0 + 8:n0 + 12], p[n0 + 12:n0 + 16],
            p[n0 + 16, :DEPTH * A_H].reshape(DEPTH, A_H))


def kernel(x, c, w_ada, b_ada, w_in, sinks, w_a, w_b, w_o, ln1_g, ln1_b, w_gate, w_up, w_down, ln2_g, ln2_b, loss_target, m_w_ada, m_b_ada, m_w_in, m_sinks, m_w_a, m_w_b, m_w_o, m_ln1_g, m_ln1_b, m_w_gate, m_w_up, m_w_down, m_ln2_g, m_ln2_b, v_w_ada, v_b_ada, v_w_in, v_sinks, v_w_a, v_w_b, v_w_o, v_ln1_g, v_ln1_b, v_w_gate, v_w_up, v_w_down, v_ln2_g, v_ln2_b):
    s_len = x.shape[1]
    tm = 256
    xi, yi, ci = _my_pos()
    chip = 2 * xi + yi
    dev = 4 * xi + 2 * yi + ci
    n_ada = w_ada.shape[2]

    c_all = _all_gather_small(jnp.concatenate([c, jnp.zeros((7, D), F32)]))[:, 0]
    b_cols = lax.dynamic_slice_in_dim(b_ada, chip * n_ada, n_ada, axis=1)
    mod_cols = _ada_forward(c_all, w_ada, jnp.broadcast_to(b_cols[:, None, :], (DEPTH, N_DEV, n_ada)))
    mod_all = _all_gather_small(mod_cols.reshape(DEPTH * N_DEV, n_ada))
    mod_all = mod_all.reshape(N_CHIPS, 2, DEPTH, N_DEV, n_ada)[:, 0]
    mod_mine = lax.dynamic_index_in_dim(mod_all, dev, axis=2, keepdims=False)
    mod = jnp.transpose(mod_mine, (1, 0, 2)).reshape(DEPTH, N_CHIPS * n_ada)

    big = (w_in, w_a, w_b, w_o, w_gate, w_up, w_down)
    gathered = []
    for l in range(DEPTH):
        g_in, g_a, g_b, g_o, g_g, g_u, g_d = _gather_weights([w[l].astype(BF16) for w in big])
        gathered.append((g_in, g_a, g_b, g_o.reshape(D, D), g_g, g_u, g_d))

    loss_blk, grad_x, grads, dmod, dln, dsinks = _local_step(
        x[0], loss_target[0], mod, sinks, (ln1_g, ln1_b, ln2_g, ln2_b), gathered, tm)

    stacked = [jnp.stack([grads[l][w] for l in range(DEPTH)]) for w in range(len(big))]
    recv = _scatter_grads(stacked)
    part = [_sum_slots(r.reshape(N_CHIPS, -1, r.shape[-1])) for r in recv]
    other = _swap_with_sibling(part)
    big_m = (m_w_in, m_w_a, m_w_b, m_w_o, m_w_gate, m_w_up, m_w_down)
    big_v = (v_w_in, v_w_a, v_w_b, v_w_o, v_w_gate, v_w_up, v_w_down)
    big_out = []
    for w, mm, vv, p, q in zip(big, big_m, big_v, part, other):
        shp = w.shape
        flat = lambda t: t.reshape(-1, shp[-1])
        res = _adam(flat(w), [p, q], flat(mm), flat(vv))
        big_out.append([t.reshape(shp) for t in res])

    small = _pack_small(dmod.reshape(DEPTH, 6 * D), dln[:, 0], dln[:, 1], dln[:, 2], dln[:, 3], dsinks, loss_blk[0, :1].repeat(D))
    small_all = _all_gather_small(small)
    sg, sd, sm, sv = _small_reduce_adam(
        small_all,
        _pack_small(b_ada, ln1_g, ln1_b, ln2_g, ln2_b, sinks, jnp.zeros((D,), F32)),
        _pack_small(m_b_ada, m_ln1_g, m_ln1_b, m_ln2_g, m_ln2_b, m_sinks, jnp.zeros((D,), F32)),
        _pack_small(v_b_ada, v_ln1_g, v_ln1_b, v_ln2_g, v_ln2_b, v_sinks, jnp.zeros((D,), F32)))
    loss = sg[DEPTH * 10 + 1, 0]
    g_small, d_small, m_small, v_small = _unpack_small(sg), _unpack_small(sd), _unpack_small(sm), _unpack_small(sv)

    dmod_all = small_all[:, :DEPTH * 6].reshape(N_DEV, DEPTH, 6 * D)
    dmod_cols = lax.dynamic_slice_in_dim(dmod_all, chip * n_ada, n_ada, axis=2)
    dmod_pad = jnp.concatenate([jnp.transpose(dmod_cols, (1, 0, 2)), jnp.zeros((DEPTH, 8, n_ada), F32)], axis=1)
    c_pad = jnp.concatenate([c_all, jnp.zeros((8, D), F32)])
    g_ada = _ada_backward(c_pad, dmod_pad)
    flat_ada = lambda t: t.reshape(-1, n_ada)
    ada_out = [t.reshape(w_ada.shape) for t in _adam(flat_ada(w_ada), [flat_ada(g_ada)], flat_ada(m_w_ada), flat_ada(v_w_ada))]

    def ordered(k):
        sm_k = (g_small, d_small, m_small, v_small)[k]
        bg = [o[k] for o in big_out]
        return [ada_out[k], sm_k[0], bg[0], sm_k[5], bg[1], bg[2], bg[3], sm_k[1], sm_k[2], bg[4], bg[5], bg[6],
                sm_k[3], sm_k[4]]

    return (loss, grad_x[None], *ordered(0), *ordered(1), *ordered(2), *ordered(3))
```

```python
import functools

import numpy as np
import jax
import jax.numpy as jnp
from jax import lax
from jax.experimental import pallas as pl
from jax.experimental.pallas import tpu as pltpu

F32 = jnp.float32
BF16 = jnp.bfloat16
MESH = pl.DeviceIdType.MESH

D = 1024
DEPTH = 4
HD = 64
A_H = 8
A_HKV = 2
A_WINDOW = 128
B_GROUPS = ((128, 1), (512, 4), (2048, 16))
B_HG = 4
N_HEADS = A_H + B_HG * len(B_GROUPS)
BLK = 128
A_QW = A_H * HD
A_KW = A_HKV * HD
B_W = B_HG * len(B_GROUPS) * HD
B_OW = B_HG * HD
QKV_W = A_QW + 2 * A_KW + 3 * B_W
IN_W = QKV_W + 2 * D
D_FF = 2816
N_CHIPS = 4
N_DEV = 8
FF_S = D_FF // N_CHIPS
IN_S = IN_W // N_CHIPS
ALPHA = (2 * DEPTH) ** 0.25
LN_EPS = 1e-5
NEG_INF = -1e30
ADAM_LR = 0.001
ADAM_B1 = 0.9
ADAM_B2 = 0.999
ADAM_EPS = 1e-08
ADAM_WD = 0.01
ADAM_STEP = 10
SMALL_ROWS = 48
VMEM_LIMIT = 56 * 1024 * 1024


def _cp(sem):
    return pltpu.CompilerParams(dimension_semantics=sem, vmem_limit_bytes=VMEM_LIMIT)


def _nt(a, b):
    return lax.dot_general(a, b, (((1,), (1,)), ((), ())), preferred_element_type=F32)


def _tn(a, b):
    return lax.dot_general(a, b, (((0,), (0,)), ((), ())), preferred_element_type=F32)


def _nn(a, b):
    return jnp.dot(a, b, preferred_element_type=F32)


def _sigmoid(x):
    return jax.nn.sigmoid(x)


def _layer_norm_rows(r, g, b):
    mu = jnp.mean(r, axis=-1, keepdims=True)
    cen = r - mu
    var = jnp.mean(cen * cen, axis=-1, keepdims=True)
    rstd = lax.rsqrt(var + LN_EPS)
    xhat = cen * rstd
    return xhat, rstd, xhat * g + b


def _layer_norm_bwd_rows(dy, xhat, rstd, g):
    dxh = dy * g
    m1 = jnp.mean(dxh, axis=-1, keepdims=True)
    m2 = jnp.mean(dxh * xhat, axis=-1, keepdims=True)
    return rstd * (dxh - m1 - xhat * m2)


def _colsum(v):
    return jnp.sum(v, axis=0, keepdims=True)


def _my_pos():
    return lax.axis_index("x"), lax.axis_index("y"), lax.axis_index("c")


def _flip(v, bit):
    return 1 - v if bit else v


def _all_gather_small(v):
    rows, cols = v.shape

    def body(v_ref, o_ref, send_sems, recv_sems):
        x, y, c = _my_pos()
        me = 4 * x + 2 * y + c
        o_ref[me] = v_ref[...]
        peers = []
        for k in range(1, N_DEV):
            peers.append((_flip(x, k & 4), _flip(y, k & 2), _flip(c, k & 1)))
        sends = []
        for k, peer in enumerate(peers):
            cp = pltpu.make_async_remote_copy(
                src_ref=v_ref, dst_ref=o_ref.at[me], send_sem=send_sems.at[k], recv_sem=recv_sems.at[k],
                device_id=peer, device_id_type=MESH)
            cp.start()
            sends.append(cp)
        for k, (px, py, pc) in enumerate(peers):
            pltpu.make_async_remote_copy(
                src_ref=v_ref, dst_ref=o_ref.at[4 * px + 2 * py + pc], send_sem=send_sems.at[k],
                recv_sem=recv_sems.at[k], device_id=(px, py, pc), device_id_type=MESH).wait_recv()
        for cp in sends:
            cp.wait_send()

    return pl.pallas_call(
        body,
        out_shape=jax.ShapeDtypeStruct((N_DEV, rows, cols), F32),
        in_specs=[pl.BlockSpec(memory_space=pltpu.VMEM)],
        out_specs=pl.BlockSpec(memory_space=pltpu.VMEM),
        scratch_shapes=[pltpu.SemaphoreType.DMA((N_DEV - 1,)), pltpu.SemaphoreType.DMA((N_DEV - 1,))],
        name="all_gather_small",
    )(v)


def _chip_peers(x, y):
    return [(_flip(x, k & 2), _flip(y, k & 1)) for k in (1, 2, 3)]


def _gather_weights(shards):
    n = len(shards)

    def body(*refs):
        ins, outs = refs[:n], refs[n:2 * n]
        send_sems, recv_sems, loc_sems = refs[2 * n:]
        x, y, c = _my_pos()
        me = 2 * x + y
        local = []
        for w in range(n):
            cp = pltpu.make_async_copy(ins[w], outs[w].at[me], loc_sems.at[w])
            cp.start()
            local.append(cp)
        peers = _chip_peers(x, y)
        sends = []
        for k, (px, py) in enumerate(peers):
            for w in range(n):
                cp = pltpu.make_async_remote_copy(
                    src_ref=ins[w], dst_ref=outs[w].at[me], send_sem=send_sems.at[3 * w + k],
                    recv_sem=recv_sems.at[3 * w + k], device_id=(px, py, c), device_id_type=MESH)
                cp.start()
                sends.append(cp)
        for k, (px, py) in enumerate(peers):
            for w in range(n):
                pltpu.make_async_remote_copy(
                    src_ref=ins[w], dst_ref=outs[w].at[2 * px + py], send_sem=send_sems.at[3 * w + k],
                    recv_sem=recv_sems.at[3 * w + k], device_id=(px, py, c), device_id_type=MESH).wait_recv()
        for cp in sends:
            cp.wait_send()
        for cp in local:
            cp.wait()

    return pl.pallas_call(
        body,
        out_shape=[jax.ShapeDtypeStruct((N_CHIPS,) + s.shape, s.dtype) for s in shards],
        in_specs=[pl.BlockSpec(memory_space=pl.ANY)] * n,
        out_specs=[pl.BlockSpec(memory_space=pl.ANY)] * n,
        scratch_shapes=[pltpu.SemaphoreType.DMA((3 * n,)), pltpu.SemaphoreType.DMA((3 * n,)),
                        pltpu.SemaphoreType.DMA((n,))],
        name="gather_weights",
    )(*shards)


def _scatter_grads(grads):
    n = len(grads)

    def body(*refs):
        ins, outs = refs[:n], refs[n:2 * n]
        send_sems, recv_sems, loc_sems = refs[2 * n:]
        x, y, c = _my_pos()
        me = 2 * x + y
        local = []
        for w in range(n):
            cp = pltpu.make_async_copy(ins[w].at[:, me], outs[w].at[me], loc_sems.at[w])
            cp.start()
            local.append(cp)
        peers = _chip_peers(x, y)
        sends = []
        for k, (px, py) in enumerate(peers):
            for w in range(n):
                cp = pltpu.make_async_remote_copy(
                    src_ref=ins[w].at[:, 2 * px + py], dst_ref=outs[w].at[me], send_sem=send_sems.at[3 * w + k],
                    recv_sem=recv_sems.at[3 * w + k], device_id=(px, py, c), device_id_type=MESH)
                cp.start()
                sends.append(cp)
        for k, (px, py) in enumerate(peers):
            for w in range(n):
                pltpu.make_async_remote_copy(
                    src_ref=ins[w].at[:, me], dst_ref=outs[w].at[2 * px + py], send_sem=send_sems.at[3 * w + k],
                    recv_sem=recv_sems.at[3 * w + k], device_id=(px, py, c), device_id_type=MESH).wait_recv()
        for cp in sends:
            cp.wait_send()
        for cp in local:
            cp.wait()

    outs = []
    for g in grads:
        nl, nj, k, m = g.shape
        outs.append(jax.ShapeDtypeStruct((nj, nl, k, m), g.dtype))
    return pl.pallas_call(
        body,
        out_shape=outs,
        in_specs=[pl.BlockSpec(memory_space=pl.ANY)] * n,
        out_specs=[pl.BlockSpec(memory_space=pl.ANY)] * n,
        scratch_shapes=[pltpu.SemaphoreType.DMA((3 * n,)), pltpu.SemaphoreType.DMA((3 * n,)),
                        pltpu.SemaphoreType.DMA((n,))],
        name="scatter_grads",
    )(*grads)


def _swap_with_sibling(parts):
    n = len(parts)

    def body(*refs):
        ins, outs = refs[:n], refs[n:2 * n]
        send_sems, recv_sems = refs[2 * n:]
        x, y, c = _my_pos()
        sib = (x, y, 1 - c)
        cps = []
        for w in range(n):
            cp = pltpu.make_async_remote_copy(
                src_ref=ins[w], dst_ref=outs[w], send_sem=send_sems.at[w], recv_sem=recv_sems.at[w],
                device_id=sib, device_id_type=MESH)
            cp.start()
            cps.append(cp)
        for cp in cps:
            cp.wait_recv()
        for cp in cps:
            cp.wait_send()

    return pl.pallas_call(
        body,
        out_shape=[jax.ShapeDtypeStruct(p.shape, p.dtype) for p in parts],
        in_specs=[pl.BlockSpec(memory_space=pl.ANY)] * n,
        out_specs=[pl.BlockSpec(memory_space=pl.ANY)] * n,
        scratch_shapes=[pltpu.SemaphoreType.DMA((n,)), pltpu.SemaphoreType.DMA((n,))],
        name="swap_with_sibling",
    )(*parts)


def _row_tile(rows, cols, target_elems=512 * 1024):
    t = max(8, min(rows, (target_elems // cols) // 8 * 8))
    while rows % t:
        t -= 8
    return t


def _sum_slots(recv):
    _, rows, cols = recv.shape
    tr = _row_tile(rows, cols)

    def body(r_ref, o_ref):
        acc = r_ref[0].astype(F32) + r_ref[1].astype(F32)
        acc = acc + r_ref[2].astype(F32)
        o_ref[...] = acc + r_ref[3].astype(F32)

    return pl.pallas_call(
        body, grid=(rows // tr,),
        in_specs=[pl.BlockSpec((N_CHIPS, tr, cols), lambda i: (0, i, 0))],
        out_specs=pl.BlockSpec((tr, cols), lambda i: (i, 0)),
        out_shape=jax.ShapeDtypeStruct((rows, cols), F32),
        compiler_params=_cp(("parallel",)), name="sum_slots",
    )(recv)


def _adam_math(w, g, m, v):
    m = ADAM_B1 * m + (1.0 - ADAM_B1) * g
    v = ADAM_B2 * v + (1.0 - ADAM_B2) * (g * g)
    m_hat = m / (1.0 - ADAM_B1 ** ADAM_STEP)
    v_hat = v / (1.0 - ADAM_B2 ** ADAM_STEP)
    delta = -ADAM_LR * (m_hat / (jnp.sqrt(v_hat) + ADAM_EPS) + ADAM_WD * w)
    return delta, m, v


def _adam(w, g_parts, m, v):
    rows, cols = w.shape
    tr = _row_tile(rows, cols, 128 * 1024)
    ng = len(g_parts)

    def body(*refs):
        w_ref, m_ref, v_ref = refs[0], refs[1], refs[2]
        g_refs = refs[3:3 + ng]
        g_out, d_out, m_out, v_out = refs[3 + ng:]
        g = g_refs[0][...]
        for r in g_refs[1:]:
            g = g + r[...]
        delta, nm, nv = _adam_math(w_ref[...], g, m_ref[...], v_ref[...])
        g_out[...] = g
        d_out[...] = delta
        m_out[...] = nm
        v_out[...] = nv

    spec = pl.BlockSpec((tr, cols), lambda i: (i, 0))
    shp = jax.ShapeDtypeStruct((rows, cols), F32)
    return pl.pallas_call(
        body, grid=(rows // tr,),
        in_specs=[spec] * (3 + ng), out_specs=[spec] * 4, out_shape=[shp] * 4,
        compiler_params=_cp(("parallel",)), name="adam",
    )(w, m, v, *g_parts)


def _small_reduce_adam(gathered, w, m, v):
    _, rows, cols = gathered.shape

    def body(g_ref, w_ref, m_ref, v_ref, g_out, d_out, m_out, v_out):
        g = g_ref[0]
        for k in range(1, N_DEV):
            g = g + g_ref[k]
        delta, nm, nv = _adam_math(w_ref[...], g, m_ref[...], v_ref[...])
        g_out[...] = g
        d_out[...] = delta
        m_out[...] = nm
        v_out[...] = nv

    shp = jax.ShapeDtypeStruct((rows, cols), F32)
    return pl.pallas_call(body, out_shape=[shp] * 4, name="small_reduce_adam")(gathered, w, m, v)


def _modulate(x, modv, row_s, row_sh, tm):
    s_len, dm = x.shape

    def body(x_ref, mod_ref, u_ref):
        s = mod_ref[row_s:row_s + 1, :]
        sh = mod_ref[row_sh:row_sh + 1, :]
        u_ref[...] = (x_ref[...] * (1.0 + s) + sh).astype(BF16)

    return pl.pallas_call(
        body, grid=(s_len // tm,),
        in_specs=[pl.BlockSpec((tm, dm), lambda i: (i, 0)), pl.BlockSpec((8, dm), lambda i: (0, 0))],
        out_specs=pl.BlockSpec((tm, dm), lambda i: (i, 0)),
        out_shape=jax.ShapeDtypeStruct((s_len, dm), BF16),
        compiler_params=_cp(("parallel",)), name="modulate",
    )(x, modv)


def _loss_head(y, target, tm):
    s_len, dm = y.shape

    def body(y_ref, t_ref, dy_ref, l_ref):
        i = pl.program_id(0)

        @pl.when(i == 0)
        def _():
            l_ref[...] = jnp.zeros_like(l_ref)

        diff = y_ref[...] - t_ref[...]
        dy_ref[...] = diff / dm
        per_tok = jnp.mean(diff * diff, axis=-1, keepdims=True)
        l_ref[...] += 0.5 * jnp.sum(per_tok, axis=0, keepdims=True)

    return pl.pallas_call(
        body, grid=(s_len // tm,),
        in_specs=[pl.BlockSpec((tm, dm), lambda i: (i, 0))] * 2,
        out_specs=[pl.BlockSpec((tm, dm), lambda i: (i, 0)), pl.BlockSpec((8, 128), lambda i: (0, 0))],
        out_shape=[jax.ShapeDtypeStruct((s_len, dm), F32), jax.ShapeDtypeStruct((8, 128), F32)],
        compiler_params=_cp(("arbitrary",)), name="loss_head",
    )(y, target)


def _ada_forward(c_all, w_ada, b_cols):
    nl, dm, n = w_ada.shape

    def body(c_ref, w_ref, b_ref, o_ref):
        cv = c_ref[...]
        sc = (cv * _sigmoid(cv)).astype(BF16)
        o_ref[...] = _nn(sc, w_ref[...].astype(BF16)) + b_ref[...]

    return pl.pallas_call(
        body, grid=(nl,),
        in_specs=[pl.BlockSpec((N_DEV, dm), lambda l: (0, 0)),
                  pl.BlockSpec((None, dm, n), lambda l: (l, 0, 0)),
                  pl.BlockSpec((None, N_DEV, n), lambda l: (l, 0, 0))],
        out_specs=pl.BlockSpec((None, N_DEV, n), lambda l: (l, 0, 0)),
        out_shape=jax.ShapeDtypeStruct((nl, N_DEV, n), F32),
        compiler_params=_cp(("parallel",)), name="ada_forward",
    )(c_all, w_ada, b_cols)


def _ada_backward(c_pad, dmod_pad):
    nl, npad, n = dmod_pad.shape
    dm = c_pad.shape[1]

    def body(c_ref, d_ref, o_ref):
        cv = c_ref[...]
        sc = (cv * _sigmoid(cv)).astype(BF16)
        o_ref[...] = _tn(sc, d_ref[...].astype(BF16))

    return pl.pallas_call(
        body, grid=(nl,),
        in_specs=[pl.BlockSpec((npad, dm), lambda l: (0, 0)),
                  pl.BlockSpec((None, npad, n), lambda l: (l, 0, 0))],
        out_specs=pl.BlockSpec((None, dm, n), lambda l: (l, 0, 0)),
        out_shape=jax.ShapeDtypeStruct((nl, dm, n), F32),
        compiler_params=_cp(("parallel",)), name="ada_backward",
    )(c_pad, dmod_pad)


def _in_proj(u, w3, t_lo, n_t, tn, out_dtype, tm, name):
    s_len, kd = u.shape
    n = w3.shape[2]
    tps = n // tn

    def body(u_ref, w_ref, o_ref):
        o_ref[...] = _nn(u_ref[...], w_ref[...]).astype(out_dtype)

    return pl.pallas_call(
        body, grid=(s_len // tm, n_t),
        in_specs=[pl.BlockSpec((tm, kd), lambda i, t: (i, 0)),
                  pl.BlockSpec((None, kd, tn), lambda i, t: ((t + t_lo) // tps, 0, (t + t_lo) % tps))],
        out_specs=pl.BlockSpec((tm, tn), lambda i, t: (i, t)),
        out_shape=jax.ShapeDtypeStruct((s_len, n_t * tn), out_dtype),
        compiler_params=_cp(("parallel", "arbitrary")), name=name,
    )(u, w3)


def _mix_out(ya, o_g, lse_g, gates, x_in, modv, lnv, w_a, w_b, w_o, tm):
    s_len = ya.shape[0]

    def body(ya_ref, o0, o1, o2, l0, l1, l2, g_ref, x_ref, mod_ref, ln_ref, wa_ref, wb_ref, wo_ref,
             xm_ref, xhat_ref, rstd_ref, u2_ref, t1_ref, yb_ref):
        la, lb, lc = l0[...], l1[...], l2[...]
        mx = jnp.maximum(jnp.maximum(la, lb), lc)
        ea, eb, ec = jnp.exp(la - mx), jnp.exp(lb - mx), jnp.exp(lc - mx)
        den = ea + eb + ec
        yb = (o0[...] * (ea / den) + o1[...] * (eb / den) + o2[...] * (ec / den)).astype(BF16)
        yb_ref[...] = yb
        yav = ya_ref[...]
        za = jnp.concatenate([_nn(yav, wa_ref[j]) for j in range(N_CHIPS)], axis=1)
        zb = jnp.concatenate([_nn(yb, wb_ref[j]) for j in range(N_CHIPS)], axis=1)
        merged = _sigmoid(g_ref[:, 0:D]) * za + _sigmoid(g_ref[:, D:2 * D]) * zb
        t1 = _nn(merged.astype(BF16), wo_ref[...])
        t1_ref[...] = t1.astype(BF16)
        r = ALPHA * x_ref[...] + mod_ref[2:3, :] * t1
        xhat, rstd, xm = _layer_norm_rows(r, ln_ref[0:1, :], ln_ref[1:2, :])
        xhat_ref[...] = xhat
        rstd_ref[...] = jnp.broadcast_to(rstd, (tm, 128))
        xm_ref[...] = xm
        u2_ref[...] = (xm * (1.0 + mod_ref[4:5, :]) + mod_ref[3:4, :]).astype(BF16)

    tok = lambda w: pl.BlockSpec((tm, w), lambda i: (i, 0))
    full = lambda shp: pl.BlockSpec(shp, lambda i: (0,) * len(shp))
    return pl.pallas_call(
        body, grid=(s_len // tm,),
        in_specs=[tok(A_QW)] + [tok(B_OW)] * 6 + [tok(2 * D), tok(D), full((8, D)), full((8, D)),
                  full(w_a.shape), full(w_b.shape), full(w_o.shape)],
        out_specs=[tok(D), tok(D), tok(128), tok(D), tok(D), tok(B_OW)],
        out_shape=[jax.ShapeDtypeStruct((s_len, D), F32), jax.ShapeDtypeStruct((s_len, D), F32),
                   jax.ShapeDtypeStruct((s_len, 128), F32), jax.ShapeDtypeStruct((s_len, D), BF16),
                   jax.ShapeDtypeStruct((s_len, D), BF16), jax.ShapeDtypeStruct((s_len, B_OW), BF16)],
        compiler_params=_cp(("parallel",)), name="mix_out",
    )(ya, *o_g, *lse_g, gates, x_in, modv, lnv, w_a, w_b, w_o)


def _ffn_forward(u2, x_mid, modv, lnv, mod_next, w_g, w_u, w_d, tm):
    s_len = u2.shape[0]

    def body(u_ref, xm_ref, mod_ref, ln_ref, nxt_ref, wg_ref, wu_ref, wd_ref,
             a_ref, b_ref, t2_ref, xhat_ref, rstd_ref, xo_ref, un_ref, acc_ref):
        j = pl.program_id(1)

        @pl.when(j == 0)
        def _():
            acc_ref[...] = jnp.zeros_like(acc_ref)

        uv = u_ref[...]
        a = _nn(uv, wg_ref[...])
        b = _nn(uv, wu_ref[...])
        a_ref[...] = a
        b_ref[...] = b
        f = (a * _sigmoid(a)) * b
        acc_ref[...] += _nn(f.astype(BF16), wd_ref[...])

        @pl.when(j == N_CHIPS - 1)
        def _():
            t2 = acc_ref[...]
            t2_ref[...] = t2.astype(BF16)
            r = ALPHA * xm_ref[...] + mod_ref[5:6, :] * t2
            xhat, rstd, xo = _layer_norm_rows(r, ln_ref[2:3, :], ln_ref[3:4, :])
            xhat_ref[...] = xhat
            rstd_ref[...] = jnp.broadcast_to(rstd, (tm, 128))
            xo_ref[...] = xo
            un_ref[...] = (xo * (1.0 + nxt_ref[1:2, :]) + nxt_ref[0:1, :]).astype(BF16)

    tok = lambda w: pl.BlockSpec((tm, w), lambda i, j: (i, 0))
    full = lambda shp: pl.BlockSpec(shp, lambda i, j: (0,) * len(shp))
    hid = pl.BlockSpec((None, tm, FF_S), lambda i, j: (j, i, 0))
    return pl.pallas_call(
        body, grid=(s_len // tm, N_CHIPS),
        in_specs=[tok(D), tok(D), full((8, D)), full((8, D)), full((8, D)),
                  pl.BlockSpec((None, D, FF_S), lambda i, j: (j, 0, 0)),
                  pl.BlockSpec((None, D, FF_S), lambda i, j: (j, 0, 0)),
                  pl.BlockSpec((None, FF_S, D), lambda i, j: (j, 0, 0))],
        out_specs=[hid, hid, tok(D), tok(D), tok(128), tok(D), tok(D)],
        out_shape=[jax.ShapeDtypeStruct((N_CHIPS, s_len, FF_S), F32),
                   jax.ShapeDtypeStruct((N_CHIPS, s_len, FF_S), F32),
                   jax.ShapeDtypeStruct((s_len, D), BF16), jax.ShapeDtypeStruct((s_len, D), F32),
                   jax.ShapeDtypeStruct((s_len, 128), F32), jax.ShapeDtypeStruct((s_len, D), F32),
                   jax.ShapeDtypeStruct((s_len, D), BF16)],
        scratch_shapes=[pltpu.VMEM((tm, D), F32)],
        compiler_params=_cp(("parallel", "arbitrary")), name="ffn_forward",
    )(u2, x_mid, modv, lnv, mod_next, w_g, w_u, w_d)


def _ffn_backward(dxo, xhat2, rstd2, t2, x_mid, a, b, modv, lnv, w_g, w_u, w_d, tm):
    s_len = dxo.shape[0]

    def body(dxo_ref, xhat_ref, rstd_ref, t2_ref, xm_ref, a_ref, b_ref, mod_ref, ln_ref, wg_ref, wu_ref, wd_ref,
             dxm_ref, dt2_ref, da_ref, db_ref, f_ref, small_ref, dr_s, dt_s, acc_ref):
        i, j = pl.program_id(0), pl.program_id(1)

        @pl.when((i == 0) & (j == 0))
        def _():
            small_ref[...] = jnp.zeros_like(small_ref)

        @pl.when(j == 0)
        def _():
            dxov, xhat = dxo_ref[...], xhat_ref[...]
            dr = _layer_norm_bwd_rows(dxov, xhat, rstd_ref[:, 0:1], ln_ref[2:3, :])
            small_ref[0:1, :] += _colsum(dxov * xhat)
            small_ref[1:2, :] += _colsum(dxov)
            small_ref[2:3, :] += _colsum(dr * t2_ref[...].astype(F32))
            dt = (mod_ref[5:6, :] * dr).astype(BF16)
            dt2_ref[...] = dt
            dt_s[...] = dt
            dr_s[...] = dr
            acc_ref[...] = jnp.zeros_like(acc_ref)

        df = _nt(dt_s[...], wd_ref[...])
        av, bv = a_ref[...], b_ref[...]
        sa = _sigmoid(av)
        sl = av * sa
        da = (df * bv * (sa * (1.0 + av * (1.0 - sa)))).astype(BF16)
        db = (df * sl).astype(BF16)
        da_ref[...] = da
        db_ref[...] = db
        f_ref[...] = (sl * bv).astype(BF16)
        acc_ref[...] += _nt(da, wg_ref[...]) + _nt(db, wu_ref[...])

        @pl.when(j == N_CHIPS - 1)
        def _():
            du = acc_ref[...]
            dxm_ref[...] = ALPHA * dr_s[...] + du * (1.0 + mod_ref[4:5, :])
            small_ref[3:4, :] += _colsum(du * xm_ref[...])
            small_ref[4:5, :] += _colsum(du)

    tok = lambda w: pl.BlockSpec((tm, w), lambda i, j: (i, 0))
    full = lambda shp: pl.BlockSpec(shp, lambda i, j: (0,) * len(shp))
    hid = pl.BlockSpec((None, tm, FF_S), lambda i, j: (j, i, 0))
    hid_bf = jax.ShapeDtypeStruct((N_CHIPS, s_len, FF_S), BF16)
    return pl.pallas_call(
        body, grid=(s_len // tm, N_CHIPS),
        in_specs=[tok(D), tok(D), tok(128), tok(D), tok(D), hid, hid, full((8, D)), full((8, D)),
                  pl.BlockSpec((None, D, FF_S), lambda i, j: (j, 0, 0)),
                  pl.BlockSpec((None, D, FF_S), lambda i, j: (j, 0, 0)),
                  pl.BlockSpec((None, FF_S, D), lambda i, j: (j, 0, 0))],
        out_specs=[tok(D), tok(D), hid, hid, hid, full((8, D))],
        out_shape=[jax.ShapeDtypeStruct((s_len, D), F32), jax.ShapeDtypeStruct((s_len, D), BF16),
                   hid_bf, hid_bf, hid_bf, jax.ShapeDtypeStruct((8, D), F32)],
        scratch_shapes=[pltpu.VMEM((tm, D), F32), pltpu.VMEM((tm, D), BF16), pltpu.VMEM((tm, D), F32)],
        compiler_params=_cp(("arbitrary", "arbitrary")), name="ffn_backward",
    )(dxo, xhat2, rstd2, t2, x_mid, a, b, modv, lnv, w_g, w_u, w_d)


def _mix_backward(dxm, xhat1, rstd1, t1, gates, ya, yb, modv, lnv, w_a, w_b, w_o, tm):
    s_len = dxm.shape[0]

    def body(dxm_ref, xhat_ref, rstd_ref, t1_ref, g_ref, ya_ref, yb_ref, mod_ref, ln_ref, wa_ref, wb_ref, wo_ref,
             dxi_ref, dt1_ref, mg_ref, dza_ref, dzb_ref, dg_ref, dya_ref, dyb_ref, small_ref):
        i = pl.program_id(0)

        @pl.when(i == 0)
        def _():
            small_ref[...] = jnp.zeros_like(small_ref)

        dxmv, xhat = dxm_ref[...], xhat_ref[...]
        dr = _layer_norm_bwd_rows(dxmv, xhat, rstd_ref[:, 0:1], ln_ref[0:1, :])
        small_ref[0:1, :] += _colsum(dxmv * xhat)
        small_ref[1:2, :] += _colsum(dxmv)
        small_ref[2:3, :] += _colsum(dr * t1_ref[...].astype(F32))
        dxi_ref[...] = ALPHA * dr
        dt1 = (mod_ref[2:3, :] * dr).astype(BF16)
        dt1_ref[...] = dt1
        dmg = _nt(dt1, wo_ref[...])
        yav, ybv = ya_ref[...], yb_ref[...]
        za = jnp.concatenate([_nn(yav, wa_ref[j]) for j in range(N_CHIPS)], axis=1)
        zb = jnp.concatenate([_nn(ybv, wb_ref[j]) for j in range(N_CHIPS)], axis=1)
        sga, sgb = _sigmoid(g_ref[:, 0:D]), _sigmoid(g_ref[:, D:2 * D])
        mg_ref[...] = (sga * za + sgb * zb).astype(BF16)
        dza = (dmg * sga).astype(BF16)
        dzb = (dmg * sgb).astype(BF16)
        dza_ref[...] = dza
        dzb_ref[...] = dzb
        dg_ref[:, 0:D] = (dmg * za * (sga * (1.0 - sga))).astype(BF16)
        dg_ref[:, D:2 * D] = (dmg * zb * (sgb * (1.0 - sgb))).astype(BF16)
        cw = D // N_CHIPS
        dya = _nt(dza[:, 0:cw], wa_ref[0])
        dyb = _nt(dzb[:, 0:cw], wb_ref[0])
        for j in range(1, N_CHIPS):
            dya = dya + _nt(dza[:, j * cw:(j + 1) * cw], wa_ref[j])
            dyb = dyb + _nt(dzb[:, j * cw:(j + 1) * cw], wb_ref[j])
        dya_ref[...] = dya.astype(BF16)
        dyb_ref[...] = dyb

    tok = lambda w: pl.BlockSpec((tm, w), lambda i: (i, 0))
    full = lambda shp: pl.BlockSpec(shp, lambda i: (0,) * len(shp))
    sd = lambda w, dt: jax.ShapeDtypeStruct((s_len, w), dt)
    return pl.pallas_call(
        body, grid=(s_len // tm,),
        in_specs=[tok(D), tok(D), tok(128), tok(D), tok(2 * D), tok(A_QW), tok(B_OW), full((8, D)), full((8, D)),
                  full(w_a.shape), full(w_b.shape), full(w_o.shape)],
        out_specs=[tok(D), tok(D), tok(D), tok(D), tok(D), tok(2 * D), tok(A_QW), tok(B_OW), full((8, D))],
        out_shape=[sd(D, F32), sd(D, BF16), sd(D, BF16), sd(D, BF16), sd(D, BF16), sd(2 * D, BF16),
                   sd(A_QW, BF16), sd(B_OW, F32), jax.ShapeDtypeStruct((8, D), F32)],
        compiler_params=_cp(("arbitrary",)), name="mix_backward",
    )(dxm, xhat1, rstd1, t1, gates, ya, yb, modv, lnv, w_a, w_b, w_o)


def _split3(v):
    hi = v.astype(BF16)
    r1 = v - hi.astype(F32)
    mid = r1.astype(BF16)
    lo = (r1 - mid.astype(F32)).astype(BF16)
    return hi, mid, lo


def _group_mix_backward(dyb, o_g, lse_g, tm):
    s_len = dyb.shape[0]

    def body(dyb_ref, o0, o1, o2, l0, l1, l2, do0, do1, do2, dl0, dl1, dl2):
        rr = lax.shift_right_logical(lax.broadcasted_iota(jnp.int32, (B_OW, B_OW), 0), 6)
        cc = lax.shift_right_logical(lax.broadcasted_iota(jnp.int32, (B_OW, B_OW), 1), 6)
        ones_bd = jnp.where(rr == cc, 1.0, 0.0).astype(BF16)

        def head_sum(v):
            hi, mid, lo = _split3(v)
            return _nn(hi, ones_bd) + _nn(mid, ones_bd) + _nn(lo, ones_bd)

        la, lb, lc = l0[...], l1[...], l2[...]
        mx = jnp.maximum(jnp.maximum(la, lb), lc)
        ea, eb, ec = jnp.exp(la - mx), jnp.exp(lb - mx), jnp.exp(lc - mx)
        den = ea + eb + ec
        wts = (ea / den, eb / den, ec / den)
        dy = dyb_ref[...]
        dws = [head_sum(dy * o[...]) for o in (o0, o1, o2)]
        dot = wts[0] * dws[0] + wts[1] * dws[1] + wts[2] * dws[2]
        for wt, dw, do_ref, dl_ref in zip(wts, dws, (do0, do1, do2), (dl0, dl1, dl2)):
            do_ref[...] = (dy * wt).astype(BF16)
            dl_ref[...] = wt * (dw - dot)

    tok = pl.BlockSpec((tm, B_OW), lambda i: (i, 0))
    return pl.pallas_call(
        body, grid=(s_len // tm,),
        in_specs=[tok] * 7, out_specs=[tok] * 6,
        out_shape=[jax.ShapeDtypeStruct((s_len, B_OW), BF16)] * 3 + [jax.ShapeDtypeStruct((s_len, B_OW), F32)] * 3,
        compiler_params=_cp(("parallel",)), name="group_mix_backward",
    )(dyb, *o_g, *lse_g)


def _in_proj_backward(dh, w_in, dxi, x_in, modv, tm):
    s_len = dh.shape[0]

    def body(dh_ref, w_ref, dxi_ref, x_ref, mod_ref, dx_ref, small_ref, acc_ref):
        i, j = pl.program_id(0), pl.program_id(1)

        @pl.when((i == 0) & (j == 0))
        def _():
            small_ref[...] = jnp.zeros_like(small_ref)

        @pl.when(j == 0)
        def _():
            acc_ref[...] = jnp.zeros_like(acc_ref)

        acc_ref[...] += _nt(dh_ref[...], w_ref[...])

        @pl.when(j == N_CHIPS - 1)
        def _():
            du = acc_ref[...]
            dx_ref[...] = dxi_ref[...] + du * (1.0 + mod_ref[1:2, :])
            small_ref[0:1, :] += _colsum(du * x_ref[...])
            small_ref[1:2, :] += _colsum(du)

    tok = lambda w: pl.BlockSpec((tm, w), lambda i, j: (i, 0))
    return pl.pallas_call(
        body, grid=(s_len // tm, N_CHIPS),
        in_specs=[pl.BlockSpec((tm, IN_S), lambda i, j: (i, j)),
                  pl.BlockSpec((None, D, IN_S), lambda i, j: (j, 0, 0)),
                  tok(D), tok(D), pl.BlockSpec((8, D), lambda i, j: (0, 0))],
        out_specs=[tok(D), pl.BlockSpec((8, D), lambda i, j: (0, 0))],
        out_shape=[jax.ShapeDtypeStruct((s_len, D), F32), jax.ShapeDtypeStruct((8, D), F32)],
        scratch_shapes=[pltpu.VMEM((tm, D), F32)],
        compiler_params=_cp(("arbitrary", "arbitrary")), name="in_proj_backward",
    )(dh, w_in, dxi, x_in, modv)


def _weight_grad(a, b, *, a_block, a_map, b_block, b_map, out_shape, out_block, out_map, n_panels, s_len, ts, name):
    acc_shape = tuple(d for d in out_block if d is not None)

    def body(a_ref, b_ref, o_ref, acc_ref):
        s = pl.program_id(1)

        @pl.when(s == 0)
        def _():
            acc_ref[...] = jnp.zeros_like(acc_ref)

        acc_ref[...] += _tn(a_ref[...], b_ref[...])

        @pl.when(s == s_len // ts - 1)
        def _():
            o_ref[...] = acc_ref[...].astype(BF16)

    return pl.pallas_call(
        body, grid=(n_panels, s_len // ts),
        in_specs=[pl.BlockSpec(a_block, a_map), pl.BlockSpec(b_block, b_map)],
        out_specs=pl.BlockSpec(out_block, out_map),
        out_shape=jax.ShapeDtypeStruct(out_shape, BF16),
        scratch_shapes=[pltpu.VMEM(acc_shape, F32)],
        compiler_params=_cp(("parallel", "arbitrary")), name=name,
    )(a, b)


def _bias_tables(slopes, max_dist, stride):
    qi = np.arange(BLK)[:, None]
    sj = np.arange(2 * BLK)[None, :]
    dist = qi + BLK - sj
    valid = (dist >= 0) & (dist <= max_dist)
    bias = -(jnp.asarray(slopes, F32).reshape(-1, 1, 1) * jnp.asarray(dist * stride, F32))
    gen = jnp.where(valid[None], bias, NEG_INF)
    first = jnp.where((valid & (sj >= BLK))[None], bias, NEG_INF)
    return jnp.stack([gen, first]).astype(F32)


def _attn_specs(dil, n_heads, n_kv, q_off, k_off, v_off, nb, total_w):
    qw, kw = n_heads * HD, n_kv * HD
    qs, ks = total_w // qw, total_w // kw
    q_spec = pl.BlockSpec((BLK, qw), lambda r, n: (jnp.minimum(n, nb - 1), r * qs + q_off // qw))
    kp = pl.BlockSpec((BLK, kw), lambda r, n: (jnp.maximum(n - 1, 0), r * ks + k_off // kw))
    kc = pl.BlockSpec((BLK, kw), lambda r, n: (jnp.minimum(n, nb - 1), r * ks + k_off // kw))
    vp = pl.BlockSpec((BLK, kw), lambda r, n: (jnp.maximum(n - 1, 0), r * ks + v_off // kw))
    vc = pl.BlockSpec((BLK, kw), lambda r, n: (jnp.minimum(n, nb - 1), r * ks + v_off // kw))
    return q_spec, kp, kc, vp, vc


def _attn_forward(qkv_f, bias, sinks, *, dil, n_heads, n_kv, q_off, k_off, v_off, out_dtype, name):
    seq, tw = qkv_f.shape
    total_w = tw // dil
    nb = seq // BLK
    grp = n_heads // n_kv
    qw = n_heads * HD
    has_sink = sinks is not None

    def body(*refs):
        if has_sink:
            sink_ref, refs = refs[0], refs[1:]
        q_ref, kp_ref, kc_ref, vp_ref, vc_ref, bias_ref, o_ref, lse_ref = refs
        n = pl.program_id(1)
        sel = jnp.where(n == 0, 1, 0)
        for h in range(n_heads):
            hk = h // grp
            hs, ks = slice(h * HD, (h + 1) * HD), slice(hk * HD, (hk + 1) * HD)
            qh = q_ref[:, hs] * 0.125
            sp = _nt(qh, kp_ref[:, ks]) + bias_ref[sel, h, :, 0:BLK]
            sc = _nt(qh, kc_ref[:, ks]) + bias_ref[sel, h, :, BLK:2 * BLK]
            m = jnp.maximum(jnp.max(sp, axis=-1, keepdims=True), jnp.max(sc, axis=-1, keepdims=True))
            if has_sink:
                m = jnp.maximum(m, sink_ref[0, h])
            ep, ec = jnp.exp(sp - m), jnp.exp(sc - m)
            den = jnp.sum(ep, axis=-1, keepdims=True) + jnp.sum(ec, axis=-1, keepdims=True)
            if has_sink:
                den = den + jnp.exp(sink_ref[0, h] - m)
            pv = _nn(ep.astype(BF16), vp_ref[:, ks]) + _nn(ec.astype(BF16), vc_ref[:, ks])
            o_ref[:, hs] = (pv * (1.0 / den)).astype(out_dtype)
            lse_ref[:, hs] = jnp.broadcast_to(m + jnp.log(den), (BLK, HD))

    q_spec, kp, kc, vp, vc = _attn_specs(dil, n_heads, n_kv, q_off, k_off, v_off, nb, total_w)
    o_spec = pl.BlockSpec((BLK, qw), lambda r, n: (n, r))
    in_specs = [q_spec, kp, kc, vp, vc, pl.BlockSpec(bias.shape, lambda r, n: (0, 0, 0, 0))]
    args = [qkv_f, qkv_f, qkv_f, qkv_f, qkv_f, bias]
    if has_sink:
        in_specs = [pl.BlockSpec(memory_space=pltpu.SMEM)] + in_specs
        args = [sinks] + args
    return pl.pallas_call(
        body, grid=(dil, nb), in_specs=in_specs, out_specs=[o_spec, o_spec],
        out_shape=[jax.ShapeDtypeStruct((seq, dil * qw), out_dtype), jax.ShapeDtypeStruct((seq, dil * qw), F32)],
        compiler_params=_cp(("parallel", "arbitrary")), name=name,
    )(*args)


def _attn_backward(qkv_f, do_f, lse_f, dlse_f, bias, sinks, *, dil, n_heads, n_kv, q_off, k_off, v_off, name):
    seq, tw = qkv_f.shape
    total_w = tw // dil
    nb = seq // BLK
    grp = n_heads // n_kv
    qw, kw = n_heads * HD, n_kv * HD
    has_sink = sinks is not None
    has_dlse = dlse_f is not None

    def body(*refs):
        refs = list(refs)
        sink_ref = refs.pop(0) if has_sink else None
        q_ref, kp_ref, kc_ref, vp_ref, vc_ref, do_ref, lse_ref = refs[:7]
        refs = refs[7:]
        dlse_ref = refs.pop(0) if has_dlse else None
        bias_ref, dq_ref, dk_ref, dv_ref, ds_ref, ck_ref, cv_ref = refs
        r, n = pl.program_id(0), pl.program_id(1)

        @pl.when((r == 0) & (n == 0))
        def _():
            ds_ref[...] = jnp.zeros_like(ds_ref)

        @pl.when(n == 0)
        def _():
            ck_ref[...] = jnp.zeros_like(ck_ref)
            cv_ref[...] = jnp.zeros_like(cv_ref)

        @pl.when(n < nb)
        def _():
            sel = jnp.where(n == 0, 1, 0)
            for hk in range(n_kv):
                ks = slice(hk * HD, (hk + 1) * HD)
                kpv, kcv, vpv, vcv = kp_ref[:, ks], kc_ref[:, ks], vp_ref[:, ks], vc_ref[:, ks]
                dkp = jnp.zeros((BLK, HD), F32)
                dkc = jnp.zeros((BLK, HD), F32)
                dvp = jnp.zeros((BLK, HD), F32)
                dvc = jnp.zeros((BLK, HD), F32)
                for h in range(hk * grp, (hk + 1) * grp):
                    hs = slice(h * HD, (h + 1) * HD)
                    qh = q_ref[:, hs] * 0.125
                    doh = do_ref[:, hs]
                    lse = lse_ref[:, h * HD:h * HD + 1]
                    pp = jnp.exp(_nt(qh, kpv) + bias_ref[sel, h, :, 0:BLK] - lse)
                    pc = jnp.exp(_nt(qh, kcv) + bias_ref[sel, h, :, BLK:2 * BLK] - lse)
                    dpp, dpc = _nt(doh, vpv), _nt(doh, vcv)
                    delta = (jnp.sum(pp * dpp, axis=-1, keepdims=True)
                             + jnp.sum(pc * dpc, axis=-1, keepdims=True))
                    shift = delta
                    if has_dlse:
                        shift = delta - dlse_ref[:, h * HD:h * HD + 1]
                    dsp = (pp * (dpp - shift)).astype(BF16)
                    dsc = (pc * (dpc - shift)).astype(BF16)
                    dq_ref[:, hs] = ((_nn(dsp, kpv) + _nn(dsc, kcv)) * 0.125).astype(BF16)
                    dkp = dkp + _tn(dsp, qh)
                    dkc = dkc + _tn(dsc, qh)
                    dvp = dvp + _tn(pp.astype(BF16), doh)
                    dvc = dvc + _tn(pc.astype(BF16), doh)
                    if has_sink:
                        psink = jnp.exp(sink_ref[0, h] - lse)
                        ds_ref[h:h + 1, :] += jnp.broadcast_to(-jnp.sum(psink * delta, axis=0, keepdims=True), (1, 128))
                dk_ref[:, ks] = (ck_ref[:, ks] + dkp).astype(BF16)
                dv_ref[:, ks] = (cv_ref[:, ks] + dvp).astype(BF16)
                ck_ref[:, ks] = dkc
                cv_ref[:, ks] = dvc

        @pl.when(n == nb)
        def _():
            dk_ref[...] = ck_ref[...].astype(BF16)
            dv_ref[...] = cv_ref[...].astype(BF16)

    q_spec, kp, kc, vp, vc = _attn_specs(dil, n_heads, n_kv, q_off, k_off, v_off, nb, total_w)
    qo_spec = pl.BlockSpec((BLK, qw), lambda r, n: (jnp.minimum(n, nb - 1), r))
    ko_spec = pl.BlockSpec((BLK, kw), lambda r, n: (jnp.maximum(n - 1, 0), r))
    in_specs = [q_spec, kp, kc, vp, vc, qo_spec, qo_spec]
    args = [qkv_f, qkv_f, qkv_f, qkv_f, qkv_f, do_f, lse_f]
    if has_dlse:
        in_specs.append(qo_spec)
        args.append(dlse_f)
    in_specs.append(pl.BlockSpec(bias.shape, lambda r, n: (0, 0, 0, 0)))
    args.append(bias)
    if has_sink:
        in_specs = [pl.BlockSpec(memory_space=pltpu.SMEM)] + in_specs
        args = [sinks] + args
    return pl.pallas_call(
        body, grid=(dil, nb + 1), in_specs=in_specs,
        out_specs=[qo_spec, ko_spec, ko_spec, pl.BlockSpec((8, 128), lambda r, n: (0, 0))],
        out_shape=[jax.ShapeDtypeStruct((seq, dil * qw), BF16), jax.ShapeDtypeStruct((seq, dil * kw), BF16),
                   jax.ShapeDtypeStruct((seq, dil * kw), BF16), jax.ShapeDtypeStruct((8, 128), F32)],
        scratch_shapes=[pltpu.VMEM((BLK, kw), F32), pltpu.VMEM((BLK, kw), F32)],
        compiler_params=_cp(("arbitrary", "arbitrary")), name=name,
    )(*args)


def _alibi_slopes():
    return jnp.exp2(-8.0 * jnp.arange(1, N_HEADS + 1, dtype=F32) / N_HEADS)


def _fold(v, dil):
    s_len, w = v.shape
    return v.reshape(s_len // dil, dil * w)


def _unfold(v, dil):
    seq, w = v.shape
    return v.reshape(seq * dil, w // dil)


def _b_offsets(g):
    q0 = A_QW + 2 * A_KW
    return q0 + g * B_OW, q0 + B_W + g * B_OW, q0 + 2 * B_W + g * B_OW


def _layer_forward(x_in, u1, modv, lnv, mod_next, sinks_l, wts, tabs, tm):
    w_in, w_a, w_b, w_o, w_g, w_u, w_d = wts
    tm_in = 1024 if u1.shape[0] % 1024 == 0 else tm
    qkv = _in_proj(u1, w_in, 0, QKV_W // 256, 256, BF16, tm_in, "in_proj_qkv")
    gates = _in_proj(u1, w_in, QKV_W // 256, 2 * D // 256, 256, F32, tm_in, "in_proj_gates")
    ya, lse_a = _attn_forward(qkv, tabs[0], sinks_l, dil=1, n_heads=A_H, n_kv=A_HKV, q_off=0, k_off=A_QW,
                              v_off=A_QW + A_KW, out_dtype=BF16, name="attn_a_forward")
    o_g, lse_g = [], []
    for g, (_, dil) in enumerate(B_GROUPS):
        qo, ko, vo = _b_offsets(g)
        o, l = _attn_forward(_fold(qkv, dil), tabs[1 + g], None, dil=dil, n_heads=B_HG, n_kv=B_HG,
                             q_off=qo, k_off=ko, v_off=vo, out_dtype=F32, name="attn_b%d_forward" % g)
        o_g.append(_unfold(o, dil))
        lse_g.append(_unfold(l, dil))
    x_mid, xhat1, rstd1, u2, t1, yb = _mix_out(ya, o_g, lse_g, gates, x_in, modv, lnv, w_a, w_b, w_o, tm)
    a, b, t2, xhat2, rstd2, x_out, u_next = _ffn_forward(u2, x_mid, modv, lnv, mod_next, w_g, w_u, w_d, tm)
    saved = dict(x_in=x_in, u1=u1, qkv=qkv, gates=gates, ya=ya, lse_a=lse_a, o_g=o_g, lse_g=lse_g, yb=yb,
                 x_mid=x_mid, xhat1=xhat1, rstd1=rstd1, u2=u2, t1=t1, a=a, b=b, t2=t2, xhat2=xhat2, rstd2=rstd2)
    return x_out, u_next, saved


def _layer_backward(dxo, sv, modv, lnv, sinks_l, wts, tabs, tm):
    w_in, w_a, w_b, w_o, w_g, w_u, w_d = wts
    s_len = dxo.shape[0]
    ts = 512 if s_len % 512 == 0 else s_len
    dxm, dt2, da, db, f, small2 = _ffn_backward(dxo, sv["xhat2"], sv["rstd2"], sv["t2"], sv["x_mid"], sv["a"],
                                                sv["b"], modv, lnv, w_g, w_u, w_d, tm)
    hid_a = dict(a_block=(ts, D), a_map=lambda p, s: (s, 0), b_block=(None, ts, FF_S), b_map=lambda p, s: (p, s, 0),
                 out_shape=(N_CHIPS, D, FF_S), out_block=(None, D, FF_S), out_map=lambda p, s: (p, 0, 0),
                 n_panels=N_CHIPS, s_len=s_len, ts=ts)
    dw_g = _weight_grad(sv["u2"], da, name="dw_gate", **hid_a)
    dw_u = _weight_grad(sv["u2"], db, name="dw_up", **hid_a)
    dw_d = _weight_grad(f, dt2, a_block=(None, ts, FF_S), a_map=lambda p, s: (p, s, 0), b_block=(ts, D),
                        b_map=lambda p, s: (s, 0), out_shape=(N_CHIPS, FF_S, D), out_block=(None, FF_S, D),
                        out_map=lambda p, s: (p, 0, 0), n_panels=N_CHIPS, s_len=s_len, ts=ts, name="dw_down")
    dxi, dt1, merged, dza, dzb, dgates, dya, dyb, small1 = _mix_backward(
        dxm, sv["xhat1"], sv["rstd1"], sv["t1"], sv["gates"], sv["ya"], sv["yb"], modv, lnv, w_a, w_b, w_o, tm)
    rw = D // N_CHIPS
    dw_o = _weight_grad(merged, dt1, a_block=(ts, rw), a_map=lambda p, s: (s, p), b_block=(ts, D),
                        b_map=lambda p, s: (s, 0), out_shape=(N_CHIPS, rw, D), out_block=(None, rw, D),
                        out_map=lambda p, s: (p, 0, 0), n_panels=N_CHIPS, s_len=s_len, ts=ts, name="dw_o")
    col = dict(b_block=(ts, rw), b_map=lambda p, s: (s, p), n_panels=N_CHIPS, s_len=s_len, ts=ts)
    dw_a = _weight_grad(sv["ya"], dza, a_block=(ts, A_QW), a_map=lambda p, s: (s, 0), out_shape=(N_CHIPS, A_QW, rw),
                        out_block=(None, A_QW, rw), out_map=lambda p, s: (p, 0, 0), name="dw_a", **col)
    dw_b = _weight_grad(sv["yb"], dzb, a_block=(ts, B_OW), a_map=lambda p, s: (s, 0), out_shape=(N_CHIPS, B_OW, rw),
                        out_block=(None, B_OW, rw), out_map=lambda p, s: (p, 0, 0), name="dw_b", **col)
    qkv = sv["qkv"]
    dqa, dka, dva, dsink = _attn_backward(qkv, dya, sv["lse_a"], None, tabs[0], sinks_l, dil=1, n_heads=A_H,
                                          n_kv=A_HKV, q_off=0, k_off=A_QW, v_off=A_QW + A_KW, name="attn_a_backward")
    do_g = _group_mix_backward(dyb, sv["o_g"], sv["lse_g"], tm)
    dq_b, dk_b, dv_b = [], [], []
    for g, (_, dil) in enumerate(B_GROUPS):
        qo, ko, vo = _b_offsets(g)
        dq, dk, dv, _ = _attn_backward(_fold(qkv, dil), _fold(do_g[g], dil), _fold(sv["lse_g"][g], dil),
                                       _fold(do_g[3 + g], dil), tabs[1 + g], None, dil=dil, n_heads=B_HG, n_kv=B_HG,
                                       q_off=qo, k_off=ko, v_off=vo, name="attn_b%d_backward" % g)
        dq_b.append(_unfold(dq, dil))
        dk_b.append(_unfold(dk, dil))
        dv_b.append(_unfold(dv, dil))
    dh = jnp.concatenate([dqa, dka, dva] + dq_b + dk_b + dv_b + [dgates], axis=1)
    dw_in = _weight_grad(sv["u1"], dh, a_block=(ts, D), a_map=lambda p, s: (s, 0), b_block=(ts, IN_S),
                         b_map=lambda p, s: (s, p), out_shape=(N_CHIPS, D, IN_S), out_block=(None, D, IN_S),
                         out_map=lambda p, s: (p, 0, 0), n_panels=N_CHIPS, s_len=s_len, ts=ts, name="dw_in")
    dx_in, small0 = _in_proj_backward(dh, w_in, dxi, sv["x_in"], modv, ts)
    dmod = jnp.stack([small0[1], small0[0], small1[2], small2[4], small2[3], small2[2]])
    dln = jnp.stack([small1[0], small1[1], small2[0], small2[1]])
    grads = (dw_in, dw_a, dw_b, dw_o, dw_g, dw_u, dw_d)
    return dx_in, grads, dmod, dln, dsink[:, 0]


def _local_step(x, target, mod, sinks, ln, gathered, tm):
    slopes = _alibi_slopes()
    tabs = [_bias_tables(slopes[:A_H], A_WINDOW - 1, 1)]
    for g, (window, dil) in enumerate(B_GROUPS):
        lo = A_H + g * B_HG
        tabs.append(_bias_tables(slopes[lo:lo + B_HG], window // dil, dil))
    zeros2 = jnp.zeros((2, D), F32)
    modvs = [jnp.concatenate([mod[l].reshape(6, D), zeros2]) for l in range(DEPTH)]
    lnvs = [jnp.concatenate([jnp.stack([ln[0][l], ln[1][l], ln[2][l], ln[3][l]]), jnp.zeros((4, D), F32)])
            for l in range(DEPTH)]
    modvs.append(jnp.zeros((8, D), F32))
    u = _modulate(x, modvs[0], 1, 0, tm)
    saved = []
    for l in range(DEPTH):
        x, u, sv = _layer_forward(x, u, modvs[l], lnvs[l], modvs[l + 1], sinks[l:l + 1], gathered[l], tabs, tm)
        saved.append(sv)
    dx, loss = _loss_head(x, target, tm)
    grads, dmods, dlns, dsinks = [None] * DEPTH, [None] * DEPTH, [None] * DEPTH, [None] * DEPTH
    for l in reversed(range(DEPTH)):
        dx, grads[l], dmods[l], dlns[l], dsinks[l] = _layer_backward(
            dx, saved[l], modvs[l], lnvs[l], sinks[l:l + 1], gathered[l], tabs, tm)
    return loss, dx, grads, jnp.stack(dmods), jnp.stack(dlns), jnp.stack(dsinks)


def _pack_small(b_ada_like, ln1_g, ln1_b, ln2_g, ln2_b, sinks_like, loss_row):
    sink_row = jnp.zeros((D,), F32).at[:DEPTH * A_H].set(sinks_like.reshape(-1))
    rows = [b_ada_like.reshape(DEPTH * 6, D), ln1_g, ln1_b, ln2_g, ln2_b, sink_row[None], loss_row[None],
            jnp.zeros((SMALL_ROWS - DEPTH * 10 - 2, D), F32)]
    return jnp.concatenate(rows)


def _unpack_small(p):
    n0 = DEPTH * 6
    return (p[:n0].reshape(DEPTH, 6 * D), p[n0:n0 + 4], p[n0 + 4:n0 + 8], p[n0 + 8:n0 + 12], p[n0 + 12:n0 + 16],
            p[n0 + 16, :DEPTH * A_H].reshape(DEPTH, A_H))


def kernel(x, c, w_ada, b_ada, w_in, sinks, w_a, w_b, w_o, ln1_g, ln1_b, w_gate, w_up, w_down, ln2_g, ln2_b, loss_target, m_w_ada, m_b_ada, m_w_in, m_sinks, m_w_a, m_w_b, m_w_o, m_ln1_g, m_ln1_b, m_w_gate, m_w_up, m_w_down, m_ln2_g, m_ln2_b, v_w_ada, v_b_ada, v_w_in, v_sinks, v_w_a, v_w_b, v_w_o, v_ln1_g, v_ln1_b, v_w_gate, v_w_up, v_w_down, v_ln2_g, v_ln2_b):
    s_len = x.shape[1]
    tm = 256
    xi, yi, ci = _my_pos()
    chip = 2 * xi + yi
    dev = 4 * xi + 2 * yi + ci
    n_ada = w_ada.shape[2]

    c_all = _all_gather_small(jnp.concatenate([c, jnp.zeros((7, D), F32)]))[:, 0]
    b_cols = lax.dynamic_slice_in_dim(b_ada, chip * n_ada, n_ada, axis=1)
    mod_cols = _ada_forward(c_all, w_ada, jnp.broadcast_to(b_cols[:, None, :], (DEPTH, N_DEV, n_ada)))
    mod_all = _all_gather_small(mod_cols.reshape(DEPTH * N_DEV, n_ada))
    mod_all = mod_all.reshape(N_CHIPS, 2, DEPTH, N_DEV, n_ada)[:, 0]
    mod_mine = lax.dynamic_index_in_dim(mod_all, dev, axis=2, keepdims=False)
    mod = jnp.transpose(mod_mine, (1, 0, 2)).reshape(DEPTH, N_CHIPS * n_ada)

    big = (w_in, w_a, w_b, w_o, w_gate, w_up, w_down)
    gathered = []
    for l in range(DEPTH):
        g_in, g_a, g_b, g_o, g_g, g_u, g_d = _gather_weights([w[l].astype(BF16) for w in big])
        gathered.append((g_in, g_a, g_b, g_o.reshape(D, D), g_g, g_u, g_d))

    loss_blk, grad_x, grads, dmod, dln, dsinks = _local_step(
        x[0], loss_target[0], mod, sinks, (ln1_g, ln1_b, ln2_g, ln2_b), gathered, tm)

    stacked = [jnp.stack([grads[l][w] for l in range(DEPTH)]) for w in range(len(big))]
    recv = _scatter_grads(stacked)
    part = [_sum_slots(r.reshape(N_CHIPS, -1, r.shape[-1])) for r in recv]
    other = _swap_with_sibling(part)
    big_m = (m_w_in, m_w_a, m_w_b, m_w_o, m_w_gate, m_w_up, m_w_down)
    big_v = (v_w_in, v_w_a, v_w_b, v_w_o, v_w_gate, v_w_up, v_w_down)
    big_out = []
    for w, mm, vv, p, q in zip(big, big_m, big_v, part, other):
        shp = w.shape
        flat = lambda t: t.reshape(-1, shp[-1])
        res = _adam(flat(w), [p, q], flat(mm), flat(vv))
        big_out.append([t.reshape(shp) for t in res])

    small = _pack_small(dmod.reshape(DEPTH, 6 * D), dln[:, 0], dln[:, 1], dln[:, 2], dln[:, 3], dsinks, loss_blk[0, :1].repeat(D))
    small_all = _all_gather_small(small)
    sg, sd, sm, sv = _small_reduce_adam(
        small_all,
        _pack_small(b_ada, ln1_g, ln1_b, ln2_g, ln2_b, sinks, jnp.zeros((D,), F32)),
        _pack_small(m_b_ada, m_ln1_g, m_ln1_b, m_ln2_g, m_ln2_b, m_sinks, jnp.zeros((D,), F32)),
        _pack_small(v_b_ada, v_ln1_g, v_ln1_b, v_ln2_g, v_ln2_b, v_sinks, jnp.zeros((D,), F32)))
    loss = sg[DEPTH * 10 + 1, 0]
    g_small, d_small, m_small, v_small = _unpack_small(sg), _unpack_small(sd), _unpack_small(sm), _unpack_small(sv)

    dmod_all = small_all[:, :DEPTH * 6].reshape(N_DEV, DEPTH, 6 * D)
    dmod_cols = lax.dynamic_slice_in_dim(dmod_all, chip * n_ada, n_ada, axis=2)
    dmod_pad = jnp.concatenate([jnp.transpose(dmod_cols, (1, 0, 2)), jnp.zeros((DEPTH, 8, n_ada), F32)], axis=1)
    c_pad = jnp.concatenate([c_all, jnp.zeros((8, D), F32)])
    g_ada = _ada_backward(c_pad, dmod_pad)
    flat_ada = lambda t: t.reshape(-1, n_ada)
    ada_out = [t.reshape(w_ada.shape) for t in _adam(flat_ada(w_ada), [flat_ada(g_ada)], flat_ada(m_w_ada), flat_ada(v_w_ada))]

    def ordered(k):
        sm_k = (g_small, d_small, m_small, v_small)[k]
        bg = [o[k] for o in big_out]
        return [ada_out[k], sm_k[0], bg[0], sm_k[5], bg[1], bg[2], bg[3], sm_k[1], sm_k[2], bg[4], bg[5], bg[6],
                sm_k[3], sm_k[4]]

    return (loss, grad_x[None], *ordered(0), *ordered(1), *ordered(2), *ordered(3))
```

```python
import functools

import numpy as np
import jax
import jax.numpy as jnp
from jax import lax
from jax.experimental import pallas as pl
from jax.experimental.pallas import tpu as pltpu

F32 = jnp.float32
BF16 = jnp.bfloat16
MESH = pl.DeviceIdType.MESH

D = 1024
DEPTH = 4
HD = 64
A_H = 8
A_HKV = 2
A_WINDOW = 128
B_GROUPS = ((128, 1), (512, 4), (2048, 16))
B_HG = 4
N_HEADS = A_H + B_HG * len(B_GROUPS)
BLK = 128
A_QW = A_H * HD
A_KW = A_HKV * HD
B_W = B_HG * len(B_GROUPS) * HD
B_OW = B_HG * HD
QKV_W = A_QW + 2 * A_KW + 3 * B_W
IN_W = QKV_W + 2 * D
D_FF = 2816
N_CHIPS = 4
N_DEV = 8
FF_S = D_FF // N_CHIPS
IN_S = IN_W // N_CHIPS
ALPHA = (2 * DEPTH) ** 0.25
LN_EPS = 1e-5
NEG_INF = -1e30
ADAM_LR = 0.001
ADAM_B1 = 0.9
ADAM_B2 = 0.999
ADAM_EPS = 1e-08
ADAM_WD = 0.01
ADAM_STEP = 10
SMALL_ROWS = 48
VMEM_LIMIT = 56 * 1024 * 1024


def _cp(sem):
    return pltpu.CompilerParams(dimension_semantics=sem, vmem_limit_bytes=VMEM_LIMIT)


def _nt(a, b):
    return lax.dot_general(a, b, (((1,), (1,)), ((), ())), preferred_element_type=F32)


def _tn(a, b):
    return lax.dot_general(a, b, (((0,), (0,)), ((), ())), preferred_element_type=F32)


def _nn(a, b):
    return jnp.dot(a, b, preferred_element_type=F32)


def _sigmoid(x):
    return jax.nn.sigmoid(x)


def _layer_norm_rows(r, g, b):
    mu = jnp.mean(r, axis=-1, keepdims=True)
    cen = r - mu
    var = jnp.mean(cen * cen, axis=-1, keepdims=True)
    rstd = lax.rsqrt(var + LN_EPS)
    xhat = cen * rstd
    return xhat, rstd, xhat * g + b


def _layer_norm_bwd_rows(dy, xhat, rstd, g):
    dxh = dy * g
    m1 = jnp.mean(dxh, axis=-1, keepdims=True)
    m2 = jnp.mean(dxh * xhat, axis=-1, keepdims=True)
    return rstd * (dxh - m1 - xhat * m2)


def _colsum(v):
    return jnp.sum(v, axis=0, keepdims=True)


def _my_pos():
    return lax.axis_index("x"), lax.axis_index("y"), lax.axis_index("c")


def _flip(v, bit):
    return 1 - v if bit else v


def _all_gather_small(v):
    rows, cols = v.shape

    def body(v_ref, o_ref, send_sems, recv_sems):
        x, y, c = _my_pos()
        me = 4 * x + 2 * y + c
        o_ref[me] = v_ref[...]
        peers = []
        for k in range(1, N_DEV):
            peers.append((_flip(x, k & 4), _flip(y, k & 2), _flip(c, k & 1)))
        sends = []
        for k, peer in enumerate(peers):
            cp = pltpu.make_async_remote_copy(
                src_ref=v_ref, dst_ref=o_ref.at[me], send_sem=send_sems.at[k], recv_sem=recv_sems.at[k],
                device_id=peer, device_id_type=MESH)
            cp.start()
            sends.append(cp)
        for k, (px, py, pc) in enumerate(peers):
            pltpu.make_async_remote_copy(
                src_ref=v_ref, dst_ref=o_ref.at[4 * px + 2 * py + pc], send_sem=send_sems.at[k],
                recv_sem=recv_sems.at[k], device_id=(px, py, pc), device_id_type=MESH).wait_recv()
        for cp in sends:
            cp.wait_send()

    return pl.pallas_call(
        body,
        out_shape=jax.ShapeDtypeStruct((N_DEV, rows, cols), F32),
        in_specs=[pl.BlockSpec(memory_space=pltpu.VMEM)],
        out_specs=pl.BlockSpec(memory_space=pltpu.VMEM),
        scratch_shapes=[pltpu.SemaphoreType.DMA((N_DEV - 1,)), pltpu.SemaphoreType.DMA((N_DEV - 1,))],
        name="all_gather_small",
    )(v)


def _chip_peers(x, y):
    return [(_flip(x, k & 2), _flip(y, k & 1)) for k in (1, 2, 3)]


def _gather_weights(shards):
    n = len(shards)

    def body(*refs):
        ins, outs = refs[:n], refs[n:2 * n]
        send_sems, recv_sems, loc_sems = refs[2 * n:]
        x, y, c = _my_pos()
        me = 2 * x + y
        local = []
        for w in range(n):
            cp = pltpu.make_async_copy(ins[w], outs[w].at[me], loc_sems.at[w])
            cp.start()
            local.append(cp)
        peers = _chip_peers(x, y)
        sends = []
        for k, (px, py) in enumerate(peers):
            for w in range(n):
                cp = pltpu.make_async_remote_copy(
                    src_ref=ins[w], dst_ref=outs[w].at[me], send_sem=send_sems.at[3 * w + k],
                    recv_sem=recv_sems.at[3 * w + k], device_id=(px, py, c), device_id_type=MESH)
                cp.start()
                sends.append(cp)
        for k, (px, py) in enumerate(peers):
            for w in range(n):
                pltpu.make_async_remote_copy(
                    src_ref=ins[w], dst_ref=outs[w].at[2 * px + py], send_sem=send_sems.at[3 * w + k],
                    recv_sem=recv_sems.at[3 * w + k], device_id=(px, py, c), device_id_type=MESH).wait_recv()
        for cp in sends:
            cp.wait_send()
        for cp in local:
            cp.wait()

    return pl.pallas_call(
        body,
        out_shape=[jax.ShapeDtypeStruct((N_CHIPS,) + s.shape, s.dtype) for s in shards],
        in_specs=[pl.BlockSpec(memory_space=pl.ANY)] * n,
        out_specs=[pl.BlockSpec(memory_space=pl.ANY)] * n,
        scratch_shapes=[pltpu.SemaphoreType.DMA((3 * n,)), pltpu.SemaphoreType.DMA((3 * n,)),
                        pltpu.SemaphoreType.DMA((n,))],
        name="gather_weights",
    )(*shards)


def _scatter_grads(grads):
    n = len(grads)

    def body(*refs):
        ins, outs = refs[:n], refs[n:2 * n]
        send_sems, recv_sems, loc_sems = refs[2 * n:]
        x, y, c = _my_pos()
        me = 2 * x + y
        local = []
        for w in range(n):
            cp = pltpu.make_async_copy(ins[w].at[:, me], outs[w].at[me], loc_sems.at[w])
            cp.start()
            local.append(cp)
        peers = _chip_peers(x, y)
        sends = []
        for k, (px, py) in enumerate(peers):
            for w in range(n):
                cp = pltpu.make_async_remote_copy(
                    src_ref=ins[w].at[:, 2 * px + py], dst_ref=outs[w].at[me], send_sem=send_sems.at[3 * w + k],
                    recv_sem=recv_sems.at[3 * w + k], device_id=(px, py, c), device_id_type=MESH)
                cp.start()
                sends.append(cp)
        for k, (px, py) in enumerate(peers):
            for w in range(n):
                pltpu.make_async_remote_copy(
                    src_ref=ins[w].at[:, me], dst_ref=outs[w].at[2 * px + py], send_sem=send_sems.at[3 * w + k],
                    recv_sem=recv_sems.at[3 * w + k], device_id=(px, py, c), device_id_type=MESH).wait_recv()
        for cp in sends:
            cp.wait_send()
        for cp in local:
            cp.wait()

    outs = []
    for g in grads:
        nl, nj, k, m = g.shape
        outs.append(jax.ShapeDtypeStruct((nj, nl, k, m), g.dtype))
    return pl.pallas_call(
        body,
        out_shape=outs,
        in_specs=[pl.BlockSpec(memory_space=pl.ANY)] * n,
        out_specs=[pl.BlockSpec(memory_space=pl.ANY)] * n,
        scratch_shapes=[pltpu.SemaphoreType.DMA((3 * n,)), pltpu.SemaphoreType.DMA((3 * n,)),
                        pltpu.SemaphoreType.DMA((n,))],
        name="scatter_grads",
    )(*grads)


def _swap_with_sibling(parts):
    n = len(parts)

    def body(*refs):
        ins, outs = refs[:n], refs[n:2 * n]
        send_sems, recv_sems = refs[2 * n:]
        x, y, c = _my_pos()
        sib = (x, y, 1 - c)
        cps = []
        for w in range(n):
            cp = pltpu.make_async_remote_copy(
                src_ref=ins[w], dst_ref=outs[w], send_sem=send_sems.at[w], recv_sem=recv_sems.at[w],
                device_id=sib, device_id_type=MESH)
            cp.start()
            cps.append(cp)
        for cp in cps:
            cp.wait_recv()
        for cp in cps:
            cp.wait_send()

    return pl.pallas_call(
        body,
        out_shape=[jax.ShapeDtypeStruct(p.shape, p.dtype) for p in parts],
        in_specs=[pl.BlockSpec(memory_space=pl.ANY)] * n,
        out_specs=[pl.BlockSpec(memory_space=pl.ANY)] * n,
        scratch_shapes=[pltpu.SemaphoreType.DMA((n,)), pltpu.SemaphoreType.DMA((n,))],
        name="swap_with_sibling",
    )(*parts)


def _row_tile(rows, cols, target_elems=512 * 1024):
    t = max(8, min(rows, (target_elems // cols) // 8 * 8))
    while rows % t:
        t -= 8
    return t


def _sum_slots(recv):
    _, rows, cols = recv.shape
    tr = _row_tile(rows, cols)

    def body(r_ref, o_ref):
        acc = r_ref[0].astype(F32) + r_ref[1].astype(F32)
        acc = acc + r_ref[2].astype(F32)
        o_ref[...] = acc + r_ref[3].astype(F32)

    return pl.pallas_call(
        body, grid=(rows // tr,),
        in_specs=[pl.BlockSpec((N_CHIPS, tr, cols), lambda i: (0, i, 0))],
        out_specs=pl.BlockSpec((tr, cols), lambda i: (i, 0)),
        out_shape=jax.ShapeDtypeStruct((rows, cols), F32),
        compiler_params=_cp(("parallel",)), name="sum_slots",
    )(recv)


def _adam_math(w, g, m, v):
    m = ADAM_B1 * m + (1.0 - ADAM_B1) * g
    v = ADAM_B2 * v + (1.0 - ADAM_B2) * (g * g)
    m_hat = m / (1.0 - ADAM_B1 ** ADAM_STEP)
    v_hat = v / (1.0 - ADAM_B2 ** ADAM_STEP)
    delta = -ADAM_LR * (m_hat / (jnp.sqrt(v_hat) + ADAM_EPS) + ADAM_WD * w)
    return delta, m, v


def _adam(w, g_parts, m, v):
    rows, cols = w.shape
    tr = _row_tile(rows, cols, 128 * 1024)
    ng = len(g_parts)

    def body(*refs):
        w_ref, m_ref, v_ref = refs[0], refs[1], refs[2]
        g_refs = refs[3:3 + ng]
        g_out, d_out, m_out, v_out = refs[3 + ng:]
        g = g_refs[0][...]
        for r in g_refs[1:]:
            g = g + r[...]
        delta, nm, nv = _adam_math(w_ref[...], g, m_ref[...], v_ref[...])
        g_out[...] = g
        d_out[...] = delta
        m_out[...] = nm
        v_out[...] = nv

    spec = pl.BlockSpec((tr, cols), lambda i: (i, 0))
    shp = jax.ShapeDtypeStruct((rows, cols), F32)
    return pl.pallas_call(
        body, grid=(rows // tr,),
        in_specs=[spec] * (3 + ng), out_specs=[spec] * 4, out_shape=[shp] * 4,
        compiler_params=_cp(("parallel",)), name="adam",
    )(w, m, v, *g_parts)


def _small_reduce_adam(gathered, w, m, v):
    _, rows, cols = gathered.shape

    def body(g_ref, w_ref, m_ref, v_ref, g_out, d_out, m_out, v_out):
        g = g_ref[0]
        for k in range(1, N_DEV):
            g = g + g_ref[k]
        delta, nm, nv = _adam_math(w_ref[...], g, m_ref[...], v_ref[...])
        g_out[...] = g
        d_out[...] = delta
        m_out[...] = nm
        v_out[...] = nv

    shp = jax.ShapeDtypeStruct((rows, cols), F32)
    return pl.pallas_call(body, out_shape=[shp] * 4, name="small_reduce_adam")(gathered, w, m, v)


def _modulate(x, modv, row_s, row_sh, tm):
    s_len, dm = x.shape

    def body(x_ref, mod_ref, u_ref):
        s = mod_ref[row_s:row_s + 1, :]
        sh = mod_ref[row_sh:row_sh + 1, :]
        u_ref[...] = (x_ref[...] * (1.0 + s) + sh).astype(BF16)

    return pl.pallas_call(
        body, grid=(s_len // tm,),
        in_specs=[pl.BlockSpec((tm, dm), lambda i: (i, 0)), pl.BlockSpec((8, dm), lambda i: (0, 0))],
        out_specs=pl.BlockSpec((tm, dm), lambda i: (i, 0)),
        out_shape=jax.ShapeDtypeStruct((s_len, dm), BF16),
        compiler_params=_cp(("parallel",)), name="modulate",
    )(x, modv)


def _loss_head(y, target, tm):
    s_len, dm = y.shape

    def body(y_ref, t_ref, dy_ref, l_ref):
        i = pl.program_id(0)

        @pl.when(i == 0)
        def _():
            l_ref[...] = jnp.zeros_like(l_ref)

        diff = y_ref[...] - t_ref[...]
        dy_ref[...] = diff / dm
        per_tok = jnp.mean(diff * diff, axis=-1, keepdims=True)
        l_ref[...] += 0.5 * jnp.sum(per_tok, axis=0, keepdims=True)

    return pl.pallas_call(
        body, grid=(s_len // tm,),
        in_specs=[pl.BlockSpec((tm, dm), lambda i: (i, 0))] * 2,
        out_specs=[pl.BlockSpec((tm, dm), lambda i: (i, 0)), pl.BlockSpec((8, 128), lambda i: (0, 0))],
        out_shape=[jax.ShapeDtypeStruct((s_len, dm), F32), jax.ShapeDtypeStruct((8, 128), F32)],
        compiler_params=_cp(("arbitrary",)), name="loss_head",
    )(y, target)


def _ada_forward(c_all, w_ada, b_cols):
    nl, dm, n = w_ada.shape

    def body(c_ref, w_ref, b_ref, o_ref):
        cv = c_ref[...]
        sc = (cv * _sigmoid(cv)).astype(BF16)
        o_ref[...] = _nn(sc, w_ref[...].astype(BF16)) + b_ref[...]

    return pl.pallas_call(
        body, grid=(nl,),
        in_specs=[pl.BlockSpec((N_DEV, dm), lambda l: (0, 0)),
                  pl.BlockSpec((None, dm, n), lambda l: (l, 0, 0)),
                  pl.BlockSpec((None, N_DEV, n), lambda l: (l, 0, 0))],
        out_specs=pl.BlockSpec((None, N_DEV, n), lambda l: (l, 0, 0)),
        out_shape=jax.ShapeDtypeStruct((nl, N_DEV, n), F32),
        compiler_params=_cp(("parallel",)), name="ada_forward",
    )(c_all, w_ada, b_cols)


def _ada_backward(c_pad, dmod_pad):
    nl, npad, n = dmod_pad.shape
    dm = c_pad.shape[1]

    def body(c_ref, d_ref, o_ref):
        cv = c_ref[...]
        sc = (cv * _sigmoid(cv)).astype(BF16)
        o_ref[...] = _tn(sc, d_ref[...].astype(BF16))

    return pl.pallas_call(
        body, grid=(nl,),
        in_specs=[pl.BlockSpec((npad, dm), lambda l: (0, 0)),
                  pl.BlockSpec((None, npad, n), lambda l: (l, 0, 0))],
        out_specs=pl.BlockSpec((None, dm, n), lambda l: (l, 0, 0)),
        out_shape=jax.ShapeDtypeStruct((nl, dm, n), F32),
        compiler_params=_cp(("parallel",)), name="ada_backward",
    )(c_pad, dmod_pad)


def _in_proj(u, w3, t_lo, n_t, tn, out_dtype, tm, name):
    s_len, kd = u.shape
    n = w3.shape[2]
    tps = n // tn

    def body(u_ref, w_ref, o_ref):
        o_ref[...] = _nn(u_ref[...], w_ref[...]).astype(out_dtype)

    return pl.pallas_call(
        body, grid=(s_len // tm, n_t),
        in_specs=[pl.BlockSpec((tm, kd), lambda i, t: (i, 0)),
                  pl.BlockSpec((None, kd, tn), lambda i, t: ((t + t_lo) // tps, 0, (t + t_lo) % tps))],
        out_specs=pl.BlockSpec((tm, tn), lambda i, t: (i, t)),
        out_shape=jax.ShapeDtypeStruct((s_len, n_t * tn), out_dtype),
        compiler_params=_cp(("parallel", "arbitrary")), name=name,
    )(u, w3)


def _mix_out(ya, o_g, lse_g, gates, x_in, modv, lnv, w_a, w_b, w_o, tm):
    s_len = ya.shape[0]

    def body(ya_ref, o0, o1, o2, l0, l1, l2, g_ref, x_ref, mod_ref, ln_ref, wa_ref, wb_ref, wo_ref,
             xm_ref, xhat_ref, rstd_ref, u2_ref, t1_ref, yb_ref):
        la, lb, lc = l0[...], l1[...], l2[...]
        mx = jnp.maximum(jnp.maximum(la, lb), lc)
        ea, eb, ec = jnp.exp(la - mx), jnp.exp(lb - mx), jnp.exp(lc - mx)
        den = ea + eb + ec
        yb = (o0[...] * (ea / den) + o1[...] * (eb / den) + o2[...] * (ec / den)).astype(BF16)
        yb_ref[...] = yb
        yav = ya_ref[...]
        za = jnp.concatenate([_nn(yav, wa_ref[j]) for j in range(N_CHIPS)], axis=1)
        zb = jnp.concatenate([_nn(yb, wb_ref[j]) for j in range(N_CHIPS)], axis=1)
        merged = _sigmoid(g_ref[:, 0:D]) * za + _sigmoid(g_ref[:, D:2 * D]) * zb
        t1 = _nn(merged.astype(BF16), wo_ref[...])
        t1_ref[...] = t1.astype(BF16)
        r = ALPHA * x_ref[...] + mod_ref[2:3, :] * t1
        xhat, rstd, xm = _layer_norm_rows(r, ln_ref[0:1, :], ln_ref[1:2, :])
        xhat_ref[...] = xhat
        rstd_ref[...] = jnp.broadcast_to(rstd, (tm, 128))
        xm_ref[...] = xm
        u2_ref[...] = (xm * (1.0 + mod_ref[4:5, :]) + mod_ref[3:4, :]).astype(BF16)

    tok = lambda w: pl.BlockSpec((tm, w), lambda i: (i, 0))
    full = lambda shp: pl.BlockSpec(shp, lambda i: (0,) * len(shp))
    return pl.pallas_call(
        body, grid=(s_len // tm,),
        in_specs=[tok(A_QW)] + [tok(B_OW)] * 6 + [tok(2 * D), tok(D), full((8, D)), full((8, D)),
                  full(w_a.shape), full(w_b.shape), full(w_o.shape)],
        out_specs=[tok(D), tok(D), tok(128), tok(D), tok(D), tok(B_OW)],
        out_shape=[jax.ShapeDtypeStruct((s_len, D), F32), jax.ShapeDtypeStruct((s_len, D), F32),
                   jax.ShapeDtypeStruct((s_len, 128), F32), jax.ShapeDtypeStruct((s_len, D), BF16),
                   jax.ShapeDtypeStruct((s_len, D), BF16), jax.ShapeDtypeStruct((s_len, B_OW), BF16)],
        compiler_params=_cp(("parallel",)), name="mix_out",
    )(ya, *o_g, *lse_g, gates, x_in, modv, lnv, w_a, w_b, w_o)


def _ffn_forward(u2, x_mid, modv, lnv, mod_next, w_g, w_u, w_d, tm):
    s_len = u2.shape[0]

    def body(u_ref, xm_ref, mod_ref, ln_ref, nxt_ref, wg_ref, wu_ref, wd_ref,
             a_ref, b_ref, t2_ref, xhat_ref, rstd_ref, xo_ref, un_ref, acc_ref):
        j = pl.program_id(1)

        @pl.when(j == 0)
        def _():
            acc_ref[...] = jnp.zeros_like(acc_ref)

        uv = u_ref[...]
        a = _nn(uv, wg_ref[...])
        b = _nn(uv, wu_ref[...])
        a_ref[...] = a
        b_ref[...] = b
        f = (a * _sigmoid(a)) * b
        acc_ref[...] += _nn(f.astype(BF16), wd_ref[...])

        @pl.when(j == N_CHIPS - 1)
        def _():
            t2 = acc_ref[...]
            t2_ref[...] = t2.astype(BF16)
            r = ALPHA * xm_ref[...] + mod_ref[5:6, :] * t2
            xhat, rstd, xo = _layer_norm_rows(r, ln_ref[2:3, :], ln_ref[3:4, :])
            xhat_ref[...] = xhat
            rstd_ref[...] = jnp.broadcast_to(rstd, (tm, 128))
            xo_ref[...] = xo
            un_ref[...] = (xo * (1.0 + nxt_ref[1:2, :]) + nxt_ref[0:1, :]).astype(BF16)

    tok = lambda w: pl.BlockSpec((tm, w), lambda i, j: (i, 0))
    full = lambda shp: pl.BlockSpec(shp, lambda i, j: (0,) * len(shp))
    hid = pl.BlockSpec((None, tm, FF_S), lambda i, j: (j, i, 0))
    return pl.pallas_call(
        body, grid=(s_len // tm, N_CHIPS),
        in_specs=[tok(D), tok(D), full((8, D)), full((8, D)), full((8, D)),
                  pl.BlockSpec((None, D, FF_S), lambda i, j: (j, 0, 0)),
                  pl.BlockSpec((None, D, FF_S), lambda i, j: (j, 0, 0)),
                  pl.BlockSpec((None, FF_S, D), lambda i, j: (j, 0, 0))],
        out_specs=[hid, hid, tok(D), tok(D), tok(128), tok(D), tok(D)],
        out_shape=[jax.ShapeDtypeStruct((N_CHIPS, s_len, FF_S), F32),
                   jax.ShapeDtypeStruct((N_CHIPS, s_len, FF_S), F32),
                   jax.ShapeDtypeStruct((s_len, D), BF16), jax.ShapeDtypeStruct((s_len, D), F32),
                   jax.ShapeDtypeStruct((s_len, 128), F32), jax.ShapeDtypeStruct((s_len, D), F32),
                   jax.ShapeDtypeStruct((s_len, D), BF16)],
        scratch_shapes=[pltpu.VMEM((tm, D), F32)],
        compiler_params=_cp(("parallel", "arbitrary")), name="ffn_forward",
    )(u2, x_mid, modv, lnv, mod_next, w_g, w_u, w_d)


def _ffn_backward(dxo, xhat2, rstd2, t2, x_mid, a, b, modv, lnv, w_g, w_u, w_d, tm):
    s_len = dxo.shape[0]

    def body(dxo_ref, xhat_ref, rstd_ref, t2_ref, xm_ref, a_ref, b_ref, mod_ref, ln_ref, wg_ref, wu_ref, wd_ref,
             dxm_ref, dt2_ref, da_ref, db_ref, f_ref, small_ref, dr_s, dt_s, acc_ref):
        i, j = pl.program_id(0), pl.program_id(1)

        @pl.when((i == 0) & (j == 0))
        def _():
            small_ref[...] = jnp.zeros_like(small_ref)

        @pl.when(j == 0)
        def _():
            dxov, xhat = dxo_ref[...], xhat_ref[...]
            dr = _layer_norm_bwd_rows(dxov, xhat, rstd_ref[:, 0:1], ln_ref[2:3, :])
            small_ref[0:1, :] += _colsum(dxov * xhat)
            small_ref[1:2, :] += _colsum(dxov)
            small_ref[2:3, :] += _colsum(dr * t2_ref[...].astype(F32))
            dt = (mod_ref[5:6, :] * dr).astype(BF16)
            dt2_ref[...] = dt
            dt_s[...] = dt
            dr_s[...] = dr
            acc_ref[...] = jnp.zeros_like(acc_ref)

        df = _nt(dt_s[...], wd_ref[...])
        av, bv = a_ref[...], b_ref[...]
        sa = _sigmoid(av)
        sl = av * sa
        da = (df * bv * (sa * (1.0 + av * (1.0 - sa)))).astype(BF16)
        db = (df * sl).astype(BF16)
        da_ref[...] = da
        db_ref[...] = db
        f_ref[...] = (sl * bv).astype(BF16)
        acc_ref[...] += _nt(da, wg_ref[...]) + _nt(db, wu_ref[...])

        @pl.when(j == N_CHIPS - 1)
        def _():
            du = acc_ref[...]
            dxm_ref[...] = ALPHA * dr_s[...] + du * (1.0 + mod_ref[4:5, :])
            small_ref[3:4, :] += _colsum(du * xm_ref[...])
            small_ref[4:5, :] += _colsum(du)

    tok = lambda w: pl.BlockSpec((tm, w), lambda i, j: (i, 0))
    full = lambda shp: pl.BlockSpec(shp, lambda i, j: (0,) * len(shp))
    hid = pl.BlockSpec((None, tm, FF_S), lambda i, j: (j, i, 0))
    hid_bf = jax.ShapeDtypeStruct((N_CHIPS, s_len, FF_S), BF16)
    return pl.pallas_call(
        body, grid=(s_len // tm, N_CHIPS),
        in_specs=[tok(D), tok(D), tok(128), tok(D), tok(D), hid, hid, full((8, D)), full((8, D)),
                  pl.BlockSpec((None, D, FF_S), lambda i, j: (j, 0, 0)),
                  pl.BlockSpec((None, D, FF_S), lambda i, j: (j, 0, 0)),
                  pl.BlockSpec((None, FF_S, D), lambda i, j: (j, 0, 0))],
        out_specs=[tok(D), tok(D), hid, hid, hid, full((8, D))],
        out_shape=[jax.ShapeDtypeStruct((s_len, D), F32), jax.ShapeDtypeStruct((s_len, D), BF16),
                   hid_bf, hid_bf, hid_bf, jax.ShapeDtypeStruct((8, D), F32)],
        scratch_shapes=[pltpu.VMEM((tm, D), F32), pltpu.VMEM((tm, D), BF16), pltpu.VMEM((tm, D), F32)],
        compiler_params=_cp(("arbitrary", "arbitrary")), name="ffn_backward",
    )(dxo, xhat2, rstd2, t2, x_mid, a, b, modv, lnv, w_g, w_u, w_d)


def _mix_backward(dxm, xhat1, rstd1, t1, gates, ya, yb, modv, lnv, w_a, w_b, w_o, tm):
    s_len = dxm.shape[0]

    def body(dxm_ref, xhat_ref, rstd_ref, t1_ref, g_ref, ya_ref, yb_ref, mod_ref, ln_ref, wa_ref, wb_ref, wo_ref,
             dxi_ref, dt1_ref, mg_ref, dza_ref, dzb_ref, dg_ref, dya_ref, dyb_ref, small_ref):
        i = pl.program_id(0)

        @pl.when(i == 0)
        def _():
            small_ref[...] = jnp.zeros_like(small_ref)

        dxmv, xhat = dxm_ref[...], xhat_ref[...]
        dr = _layer_norm_bwd_rows(dxmv, xhat, rstd_ref[:, 0:1], ln_ref[0:1, :])
        small_ref[0:1, :] += _colsum(dxmv * xhat)
        small_ref[1:2, :] += _colsum(dxmv)
        small_ref[2:3, :] += _colsum(dr * t1_ref[...].astype(F32))
        dxi_ref[...] = ALPHA * dr
        dt1 = (mod_ref[2:3, :] * dr).astype(BF16)
        dt1_ref[...] = dt1
        dmg = _nt(dt1, wo_ref[...])
        yav, ybv = ya_ref[...], yb_ref[...]
        za = jnp.concatenate([_nn(yav, wa_ref[j]) for j in range(N_CHIPS)], axis=1)
        zb = jnp.concatenate([_nn(ybv, wb_ref[j]) for j in range(N_CHIPS)], axis=1)
        sga, sgb = _sigmoid(g_ref[:, 0:D]), _sigmoid(g_ref[:, D:2 * D])
        mg_ref[...] = (sga * za + sgb * zb).astype(BF16)
        dza = (dmg * sga).astype(BF16)
        dzb = (dmg * sgb).astype(BF16)
        dza_ref[...] = dza
        dzb_ref[...] = dzb
        dg_ref[:, 0:D] = (dmg * za * (sga * (1.0 - sga))).astype(BF16)
        dg_ref[:, D:2 * D] = (dmg * zb * (sgb * (1.0 - sgb))).astype(BF16)
        cw = D // N_CHIPS
        dya = _nt(dza[:, 0:cw], wa_ref[0])
        dyb = _nt(dzb[:, 0:cw], wb_ref[0])
        for j in range(1, N_CHIPS):
            dya = dya + _nt(dza[:, j * cw:(j + 1) * cw], wa_ref[j])
            dyb = dyb + _nt(dzb[:, j * cw:(j + 1) * cw], wb_ref[j])
        dya_ref[...] = dya.astype(BF16)
        dyb_ref[...] = dyb

    tok = lambda w: pl.BlockSpec((tm, w), lambda i: (i, 0))
    full = lambda shp: pl.BlockSpec(shp, lambda i: (0,) * len(shp))
    sd = lambda w, dt: jax.ShapeDtypeStruct((s_len, w), dt)
    return pl.pallas_call(
        body, grid=(s_len // tm,),
        in_specs=[tok(D), tok(D), tok(128), tok(D), tok(2 * D), tok(A_QW), tok(B_OW), full((8, D)), full((8, D)),
                  full(w_a.shape), full(w_b.shape), full(w_o.shape)],
        out_specs=[tok(D), tok(D), tok(D), tok(D), tok(D), tok(2 * D), tok(A_QW), tok(B_OW), full((8, D))],
        out_shape=[sd(D, F32), sd(D, BF16), sd(D, BF16), sd(D, BF16), sd(D, BF16), sd(2 * D, BF16),
                   sd(A_QW, BF16), sd(B_OW, F32), jax.ShapeDtypeStruct((8, D), F32)],
        compiler_params=_cp(("arbitrary",)), name="mix_backward",
    )(dxm, xhat1, rstd1, t1, gates, ya, yb, modv, lnv, w_a, w_b, w_o)


def _split3(v):
    hi = v.astype(BF16)
    r1 = v - hi.astype(F32)
    mid = r1.astype(BF16)
    lo = (r1 - mid.astype(F32)).astype(BF16)
    return hi, mid, lo


def _group_mix_backward(dyb, o_g, lse_g, tm):
    s_len = dyb.shape[0]

    def body(dyb_ref, o0, o1, o2, l0, l1, l2, do0, do1, do2, dl0, dl1, dl2):
        rr = lax.shift_right_logical(lax.broadcasted_iota(jnp.int32, (B_OW, B_OW), 0), 6)
        cc = lax.shift_right_logical(lax.broadcasted_iota(jnp.int32, (B_OW, B_OW), 1), 6)
        ones_bd = jnp.where(rr == cc, 1.0, 0.0).astype(BF16)

        def head_sum(v):
            hi, mid, lo = _split3(v)
            return _nn(hi, ones_bd) + _nn(mid, ones_bd) + _nn(lo, ones_bd)

        la, lb, lc = l0[...], l1[...], l2[...]
        mx = jnp.maximum(jnp.maximum(la, lb), lc)
        ea, eb, ec = jnp.exp(la - mx), jnp.exp(lb - mx), jnp.exp(lc - mx)
        den = ea + eb + ec
        wts = (ea / den, eb / den, ec / den)
        dy = dyb_ref[...]
        dws = [head_sum(dy * o[...]) for o in (o0, o1, o2)]
        dot = wts[0] * dws[0] + wts[1] * dws[1] + wts[2] * dws[2]
        for wt, dw, do_ref, dl_ref in zip(wts, dws, (do0, do1, do2), (dl0, dl1, dl2)):
            do_ref[...] = (dy * wt).astype(BF16)
            dl_ref[...] = wt * (dw - dot)

    tok = pl.BlockSpec((tm, B_OW), lambda i: (i, 0))
    return pl.pallas_call(
        body, grid=(s_len // tm,),
        in_specs=[tok] * 7, out_specs=[tok] * 6,
        out_shape=[jax.ShapeDtypeStruct((s_len, B_OW), BF16)] * 3 + [jax.ShapeDtypeStruct((s_len, B_OW), F32)] * 3,
        compiler_params=_cp(("parallel",)), name="group_mix_backward",
    )(dyb, *o_g, *lse_g)


def _in_proj_backward(dh, w_in, dxi, x_in, modv, tm):
    s_len = dh.shape[0]

    def body(dh_ref, w_ref, dxi_ref, x_ref, mod_ref, dx_ref, small_ref, acc_ref):
        i, j = pl.program_id(0), pl.program_id(1)

        @pl.when((i == 0) & (j == 0))
        def _():
            small_ref[...] = jnp.zeros_like(small_ref)

        @pl.when(j == 0)
        def _():
            acc_ref[...] = jnp.zeros_like(acc_ref)

        acc_ref[...] += _nt(dh_ref[...], w_ref[...])

        @pl.when(j == N_CHIPS - 1)
        def _():
            du = acc_ref[...]
            dx_ref[...] = dxi_ref[...] + du * (1.0 + mod_ref[1:2, :])
            small_ref[0:1, :] += _colsum(du * x_ref[...])
            small_ref[1:2, :] += _colsum(du)

    tok = lambda w: pl.BlockSpec((tm, w), lambda i, j: (i, 0))
    return pl.pallas_call(
        body, grid=(s_len // tm, N_CHIPS),
        in_specs=[pl.BlockSpec((tm, IN_S), lambda i, j: (i, j)),
                  pl.BlockSpec((None, D, IN_S), lambda i, j: (j, 0, 0)),
                  tok(D), tok(D), pl.BlockSpec((8, D), lambda i, j: (0, 0))],
        out_specs=[tok(D), pl.BlockSpec((8, D), lambda i, j: (0, 0))],
        out_shape=[jax.ShapeDtypeStruct((s_len, D), F32), jax.ShapeDtypeStruct((8, D), F32)],
        scratch_shapes=[pltpu.VMEM((tm, D), F32)],
        compiler_params=_cp(("arbitrary", "arbitrary")), name="in_proj_backward",
    )(dh, w_in, dxi, x_in, modv)


def _weight_grad(a, b, *, a_block, a_map, b_block, b_map, out_shape, out_block, out_map, n_panels, s_len, ts, name):
    acc_shape = tuple(d for d in out_block if d is not None)

    def body(a_ref, b_ref, o_ref, acc_ref):
        s = pl.program_id(1)

        @pl.when(s == 0)
        def _():
            acc_ref[...] = jnp.zeros_like(acc_ref)

        acc_ref[...] += _tn(a_ref[...], b_ref[...])

        @pl.when(s == s_len // ts - 1)
        def _():
            o_ref[...] = acc_ref[...].astype(BF16)

    return pl.pallas_call(
        body, grid=(n_panels, s_len // ts),
        in_specs=[pl.BlockSpec(a_block, a_map), pl.BlockSpec(b_block, b_map)],
        out_specs=pl.BlockSpec(out_block, out_map),
        out_shape=jax.ShapeDtypeStruct(out_shape, BF16),
        scratch_shapes=[pltpu.VMEM(acc_shape, F32)],
        compiler_params=_cp(("parallel", "arbitrary")), name=name,
    )(a, b)


def _bias_tables(slopes, max_dist, stride):
    qi = np.arange(BLK)[:, None]
    sj = np.arange(2 * BLK)[None, :]
    dist = qi + BLK - sj
    valid = (dist >= 0) & (dist <= max_dist)
    bias = -(jnp.asarray(slopes, F32).reshape(-1, 1, 1) * jnp.asarray(dist * stride, F32))
    gen = jnp.where(valid[None], bias, NEG_INF)
    first = jnp.where((valid & (sj >= BLK))[None], bias, NEG_INF)
    return jnp.stack([gen, first]).astype(F32)


def _attn_specs(dil, n_heads, n_kv, q_off, k_off, v_off, nb, total_w):
    qw, kw = n_heads * HD, n_kv * HD
    qs, ks = total_w // qw, total_w // kw
    q_spec = pl.BlockSpec((BLK, qw), lambda r, n: (jnp.minimum(n, nb - 1), r * qs + q_off // qw))
    kp = pl.BlockSpec((BLK, kw), lambda r, n: (jnp.maximum(n - 1, 0), r * ks + k_off // kw))
    kc = pl.BlockSpec((BLK, kw), lambda r, n: (jnp.minimum(n, nb - 1), r * ks + k_off // kw))
    vp = pl.BlockSpec((BLK, kw), lambda r, n: (jnp.maximum(n - 1, 0), r * ks + v_off // kw))
    vc = pl.BlockSpec((BLK, kw), lambda r, n: (jnp.minimum(n, nb - 1), r * ks + v_off // kw))
    return q_spec, kp, kc, vp, vc


def _attn_forward(qkv_f, bias, sinks, *, dil, n_heads, n_kv, q_off, k_off, v_off, out_dtype, name):
    seq, tw = qkv_f.shape
    total_w = tw // dil
    nb = seq // BLK
    grp = n_heads // n_kv
    qw = n_heads * HD
    has_sink = sinks is not None

    def body(*refs):
        if has_sink:
            sink_ref, refs = refs[0], refs[1:]
        q_ref, kp_ref, kc_ref, vp_ref, vc_ref, bias_ref, o_ref, lse_ref = refs
        n = pl.program_id(1)
        sel = jnp.where(n == 0, 1, 0)
        for h in range(n_heads):
            hk = h // grp
            hs, ks = slice(h * HD, (h + 1) * HD), slice(hk * HD, (hk + 1) * HD)
            qh = q_ref[:, hs] * 0.125
            sp = _nt(qh, kp_ref[:, ks]) + bias_ref[sel, h, :, 0:BLK]
            sc = _nt(qh, kc_ref[:, ks]) + bias_ref[sel, h, :, BLK:2 * BLK]
            m = jnp.maximum(jnp.max(sp, axis=-1, keepdims=True), jnp.max(sc, axis=-1, keepdims=True))
            if has_sink:
                m = jnp.maximum(m, sink_ref[0, h])
            ep, ec = jnp.exp(sp - m), jnp.exp(sc - m)
            den = jnp.sum(ep, axis=-1, keepdims=True) + jnp.sum(ec, axis=-1, keepdims=True)
            if has_sink:
                den = den + jnp.exp(sink_ref[0, h] - m)
            pv = _nn(ep.astype(BF16), vp_ref[:, ks]) + _nn(ec.astype(BF16), vc_ref[:, ks])
            o_ref[:, hs] = (pv * (1.0 / den)).astype(out_dtype)
            lse_ref[:, hs] = jnp.broadcast_to(m + jnp.log(den), (BLK, HD))

    q_spec, kp, kc, vp, vc = _attn_specs(dil, n_heads, n_kv, q_off, k_off, v_off, nb, total_w)
    o_spec = pl.BlockSpec((BLK, qw), lambda r, n: (n, r))
    in_specs = [q_spec, kp, kc, vp, vc, pl.BlockSpec(bias.shape, lambda r, n: (0, 0, 0, 0))]
    args = [qkv_f, qkv_f, qkv_f, qkv_f, qkv_f, bias]
    if has_sink:
        in_specs = [pl.BlockSpec(memory_space=pltpu.SMEM)] + in_specs
        args = [sinks] + args
    return pl.pallas_call(
        body, grid=(dil, nb), in_specs=in_specs, out_specs=[o_spec, o_spec],
        out_shape=[jax.ShapeDtypeStruct((seq, dil * qw), out_dtype), jax.ShapeDtypeStruct((seq, dil * qw), F32)],
        compiler_params=_cp(("parallel", "arbitrary")), name=name,
    )(*args)


def _attn_backward(qkv_f, do_f, lse_f, dlse_f, bias, sinks, *, dil, n_heads, n_kv, q_off, k_off, v_off, name):
    seq, tw = qkv_f.shape
    total_w = tw // dil
    nb = seq // BLK
    grp = n_heads // n_kv
    qw, kw = n_heads * HD, n_kv * HD
    has_sink = sinks is not None
    has_dlse = dlse_f is not None

    def body(*refs):
        refs = list(refs)
        sink_ref = refs.pop(0) if has_sink else None
        q_ref, kp_ref, kc_ref, vp_ref, vc_ref, do_ref, lse_ref = refs[:7]
        refs = refs[7:]
        dlse_ref = refs.pop(0) if has_dlse else None
        bias_ref, dq_ref, dk_ref, dv_ref, ds_ref, ck_ref, cv_ref = refs
        r, n = pl.program_id(0), pl.program_id(1)

        @pl.when((r == 0) & (n == 0))
        def _():
            ds_ref[...] = jnp.zeros_like(ds_ref)

        @pl.when(n == 0)
        def _():
            ck_ref[...] = jnp.zeros_like(ck_ref)
            cv_ref[...] = jnp.zeros_like(cv_ref)

        @pl.when(n < nb)
        def _():
            sel = jnp.where(n == 0, 1, 0)
            for hk in range(n_kv):
                ks = slice(hk * HD, (hk + 1) * HD)
                kpv, kcv, vpv, vcv = kp_ref[:, ks], kc_ref[:, ks], vp_ref[:, ks], vc_ref[:, ks]
                dkp = jnp.zeros((BLK, HD), F32)
                dkc = jnp.zeros((BLK, HD), F32)
                dvp = jnp.zeros((BLK, HD), F32)
                dvc = jnp.zeros((BLK, HD), F32)
                for h in range(hk * grp, (hk + 1) * grp):
                    hs = slice(h * HD, (h + 1) * HD)
                    qh = q_ref[:, hs] * 0.125
                    doh = do_ref[:, hs]
                    lse = lse_ref[:, h * HD:h * HD + 1]
                    pp = jnp.exp(_nt(qh, kpv) + bias_ref[sel, h, :, 0:BLK] - lse)
                    pc = jnp.exp(_nt(qh, kcv) + bias_ref[sel, h, :, BLK:2 * BLK] - lse)
                    dpp, dpc = _nt(doh, vpv), _nt(doh, vcv)
                    delta = (jnp.sum(pp * dpp, axis=-1, keepdims=True)
                             + jnp.sum(pc * dpc, axis=-1, keepdims=True))
                    shift = delta
                    if has_dlse:
                        shift = delta - dlse_ref[:, h * HD:h * HD + 1]
                    dsp = (pp * (dpp - shift)).astype(BF16)
                    dsc = (pc * (dpc - shift)).astype(BF16)
                    dq_ref[:, hs] = ((_nn(dsp, kpv) + _nn(dsc, kcv)) * 0.125).astype(BF16)
                    dkp = dkp + _tn(dsp, qh)
                    dkc = dkc + _tn(dsc, qh)
                    dvp = dvp + _tn(pp.astype(BF16), doh)
                    dvc = dvc + _tn(pc.astype(BF16), doh)
                    if has_sink:
                        psink = jnp.exp(sink_ref[0, h] - lse)
                        ds_ref[h:h + 1, :] += jnp.broadcast_to(-jnp.sum(psink * delta, axis=0, keepdims=True), (1, 128))
                dk_ref[:, ks] = (ck_ref[:, ks] + dkp).astype(BF16)
                dv_ref[:, ks] = (cv_ref[:, ks] + dvp).astype(BF16)
                ck_ref[:, ks] = dkc
                cv_ref[:, ks] = dvc

        @pl.when(n == nb)
        def _():
            dk_ref[...] = ck_ref[...].astype(BF16)
            dv_ref[...] = cv_ref[...].astype(BF16)

    q_spec, kp, kc, vp, vc = _attn_specs(dil, n_heads, n_kv, q_off, k_off, v_off, nb, total_w)
    qo_spec = pl.BlockSpec((BLK, qw), lambda r, n: (jnp.minimum(n, nb - 1), r))
    ko_spec = pl.BlockSpec((BLK, kw), lambda r, n: (jnp.maximum(n - 1, 0), r))
    in_specs = [q_spec, kp, kc, vp, vc, qo_spec, qo_spec]
    args = [qkv_f, qkv_f, qkv_f, qkv_f, qkv_f, do_f, lse_f]
    if has_dlse:
        in_specs.append(qo_spec)
        args.append(dlse_f)
    in_specs.append(pl.BlockSpec(bias.shape, lambda r, n: (0, 0, 0, 0)))
    args.append(bias)
    if has_sink:
        in_specs = [pl.BlockSpec(memory_space=pltpu.SMEM)] + in_specs
        args = [sinks] + args
    return pl.pallas_call(
        body, grid=(dil, nb + 1), in_specs=in_specs,
        out_specs=[qo_spec, ko_spec, ko_spec, pl.BlockSpec((8, 128), lambda r, n: (0, 0))],
        out_shape=[jax.ShapeDtypeStruct((seq, dil * qw), BF16), jax.ShapeDtypeStruct((seq, dil * kw), BF16),
                   jax.ShapeDtypeStruct((seq, dil * kw), BF16), jax.ShapeDtypeStruct((8, 128), F32)],
        scratch_shapes=[pltpu.VMEM((BLK, kw), F32), pltpu.VMEM((BLK, kw), F32)],
        compiler_params=_cp(("arbitrary", "arbitrary")), name=name,
    )(*args)


def _alibi_slopes():
    return jnp.exp2(-8.0 * jnp.arange(1, N_HEADS + 1, dtype=F32) / N_HEADS)


def _fold(v, dil):
    s_len, w = v.shape
    return v.reshape(s_len // dil, dil * w)


def _unfold(v, dil):
    seq, w = v.shape
    return v.reshape(seq * dil, w // dil)


def _b_offsets(g):
    q0 = A_QW + 2 * A_KW
    return q0 + g * B_OW, q0 + B_W + g * B_OW, q0 + 2 * B_W + g * B_OW


def _layer_forward(x_in, u1, modv, lnv, mod_next, sinks_l, wts, tabs, tm):
    w_in, w_a, w_b, w_o, w_g, w_u, w_d = wts
    tm_in = 1024 if u1.shape[0] % 1024 == 0 else tm
    qkv = _in_proj(u1, w_in, 0, QKV_W // 256, 256, BF16, tm_in, "in_proj_qkv")
    gates = _in_proj(u1, w_in, QKV_W // 256, 2 * D // 256, 256, F32, tm_in, "in_proj_gates")
    ya, lse_a = _attn_forward(qkv, tabs[0], sinks_l, dil=1, n_heads=A_H, n_kv=A_HKV, q_off=0, k_off=A_QW,
                              v_off=A_QW + A_KW, out_dtype=BF16, name="attn_a_forward")
    o_g, lse_g = [], []
    for g, (_, dil) in enumerate(B_GROUPS):
        qo, ko, vo = _b_offsets(g)
        o, l = _attn_forward(_fold(qkv, dil), tabs[1 + g], None, dil=dil, n_heads=B_HG, n_kv=B_HG,
                             q_off=qo, k_off=ko, v_off=vo, out_dtype=F32, name="attn_b%d_forward" % g)
        o_g.append(_unfold(o, dil))
        lse_g.append(_unfold(l, dil))
    x_mid, xhat1, rstd1, u2, t1, yb = _mix_out(ya, o_g, lse_g, gates, x_in, modv, lnv, w_a, w_b, w_o, tm)
    tm_ffn = 512 if u2.shape[0] % 512 == 0 else tm
    a, b, t2, xhat2, rstd2, x_out, u_next = _ffn_forward(u2, x_mid, modv, lnv, mod_next, w_g, w_u, w_d, tm_ffn)
    saved = dict(x_in=x_in, u1=u1, qkv=qkv, gates=gates, ya=ya, lse_a=lse_a, o_g=o_g, lse_g=lse_g, yb=yb,
                 x_mid=x_mid, xhat1=xhat1, rstd1=rstd1, u2=u2, t1=t1, a=a, b=b, t2=t2, xhat2=xhat2, rstd2=rstd2)
    return x_out, u_next, saved


def _layer_backward(dxo, sv, modv, lnv, sinks_l, wts, tabs, tm):
    w_in, w_a, w_b, w_o, w_g, w_u, w_d = wts
    s_len = dxo.shape[0]
    ts = 512 if s_len % 512 == 0 else s_len
    dxm, dt2, da, db, f, small2 = _ffn_backward(dxo, sv["xhat2"], sv["rstd2"], sv["t2"], sv["x_mid"], sv["a"],
                                                sv["b"], modv, lnv, w_g, w_u, w_d, tm)
    hid_a = dict(a_block=(ts, D), a_map=lambda p, s: (s, 0), b_block=(None, ts, FF_S), b_map=lambda p, s: (p, s, 0),
                 out_shape=(N_CHIPS, D, FF_S), out_block=(None, D, FF_S), out_map=lambda p, s: (p, 0, 0),
                 n_panels=N_CHIPS, s_len=s_len, ts=ts)
    dw_g = _weight_grad(sv["u2"], da, name="dw_gate", **hid_a)
    dw_u = _weight_grad(sv["u2"], db, name="dw_up", **hid_a)
    dw_d = _weight_grad(f, dt2, a_block=(None, ts, FF_S), a_map=lambda p, s: (p, s, 0), b_block=(ts, D),
                        b_map=lambda p, s: (s, 0), out_shape=(N_CHIPS, FF_S, D), out_block=(None, FF_S, D),
                        out_map=lambda p, s: (p, 0, 0), n_panels=N_CHIPS, s_len=s_len, ts=ts, name="dw_down")
    dxi, dt1, merged, dza, dzb, dgates, dya, dyb, small1 = _mix_backward(
        dxm, sv["xhat1"], sv["rstd1"], sv["t1"], sv["gates"], sv["ya"], sv["yb"], modv, lnv, w_a, w_b, w_o, tm)
    rw = D // N_CHIPS
    dw_o = _weight_grad(merged, dt1, a_block=(ts, rw), a_map=lambda p, s: (s, p), b_block=(ts, D),
                        b_map=lambda p, s: (s, 0), out_shape=(N_CHIPS, rw, D), out_block=(None, rw, D),
                        out_map=lambda p, s: (p, 0, 0), n_panels=N_CHIPS, s_len=s_len, ts=ts, name="dw_o")
    col = dict(b_block=(ts, rw), b_map=lambda p, s: (s, p), n_panels=N_CHIPS, s_len=s_len, ts=ts)
    dw_a = _weight_grad(sv["ya"], dza, a_block=(ts, A_QW), a_map=lambda p, s: (s, 0), out_shape=(N_CHIPS, A_QW, rw),
                        out_block=(None, A_QW, rw), out_map=lambda p, s: (p, 0, 0), name="dw_a", **col)
    dw_b = _weight_grad(sv["yb"], dzb, a_block=(ts, B_OW), a_map=lambda p, s: (s, 0), out_shape=(N_CHIPS, B_OW, rw),
                        out_block=(None, B_OW, rw), out_map=lambda p, s: (p, 0, 0), name="dw_b", **col)
    qkv = sv["qkv"]
    dqa, dka, dva, dsink = _attn_backward(qkv, dya, sv["lse_a"], None, tabs[0], sinks_l, dil=1, n_heads=A_H,
                                          n_kv=A_HKV, q_off=0, k_off=A_QW, v_off=A_QW + A_KW, name="attn_a_backward")
    do_g = _group_mix_backward(dyb, sv["o_g"], sv["lse_g"], tm)
    dq_b, dk_b, dv_b = [], [], []
    for g, (_, dil) in enumerate(B_GROUPS):
        qo, ko, vo = _b_offsets(g)
        dq, dk, dv, _ = _attn_backward(_fold(qkv, dil), _fold(do_g[g], dil), _fold(sv["lse_g"][g], dil),
                                       _fold(do_g[3 + g], dil), tabs[1 + g], None, dil=dil, n_heads=B_HG, n_kv=B_HG,
                                       q_off=qo, k_off=ko, v_off=vo, name="attn_b%d_backward" % g)
        dq_b.append(_unfold(dq, dil))
        dk_b.append(_unfold(dk, dil))
        dv_b.append(_unfold(dv, dil))
    dh = jnp.concatenate([dqa, dka, dva] + dq_b + dk_b + dv_b + [dgates], axis=1)
    dw_in = _weight_grad(sv["u1"], dh, a_block=(ts, D), a_map=lambda p, s: (s, 0), b_block=(ts, IN_S),
                         b_map=lambda p, s: (s, p), out_shape=(N_CHIPS, D, IN_S), out_block=(None, D, IN_S),
                         out_map=lambda p, s: (p, 0, 0), n_panels=N_CHIPS, s_len=s_len, ts=ts, name="dw_in")
    dx_in, small0 = _in_proj_backward(dh, w_in, dxi, sv["x_in"], modv, ts)
    dmod = jnp.stack([small0[1], small0[0], small1[2], small2[4], small2[3], small2[2]])
    dln = jnp.stack([small1[0], small1[1], small2[0], small2[1]])
    grads = (dw_in, dw_a, dw_b, dw_o, dw_g, dw_u, dw_d)
    return dx_in, grads, dmod, dln, dsink[:, 0]


def _local_step(x, target, mod, sinks, ln, gathered, tm):
    slopes = _alibi_slopes()
    tabs = [_bias_tables(slopes[:A_H], A_WINDOW - 1, 1)]
    for g, (window, dil) in enumerate(B_GROUPS):
        lo = A_H + g * B_HG
        tabs.append(_bias_tables(slopes[lo:lo + B_HG], window // dil, dil))
    zeros2 = jnp.zeros((2, D), F32)
    modvs = [jnp.concatenate([mod[l].reshape(6, D), zeros2]) for l in range(DEPTH)]
    lnvs = [jnp.concatenate([jnp.stack([ln[0][l], ln[1][l], ln[2][l], ln[3][l]]), jnp.zeros((4, D), F32)])
            for l in range(DEPTH)]
    modvs.append(jnp.zeros((8, D), F32))
    u = _modulate(x, modvs[0], 1, 0, tm)
    saved = []
    for l in range(DEPTH):
        x, u, sv = _layer_forward(x, u, modvs[l], lnvs[l], modvs[l + 1], sinks[l:l + 1], gathered[l], tabs, tm)
        saved.append(sv)
    dx, loss = _loss_head(x, target, tm)
    grads, dmods, dlns, dsinks = [None] * DEPTH, [None] * DEPTH, [None] * DEPTH, [None] * DEPTH
    for l in reversed(range(DEPTH)):
        dx, grads[l], dmods[l], dlns[l], dsinks[l] = _layer_backward(
            dx, saved[l], modvs[l], lnvs[l], sinks[l:l + 1], gathered[l], tabs, tm)
    return loss, dx, grads, jnp.stack(dmods), jnp.stack(dlns), jnp.stack(dsinks)


def _pack_small(b_ada_like, ln1_g, ln1_b, ln2_g, ln2_b, sinks_like, loss_row):
    sink_row = jnp.zeros((D,), F32).at[:DEPTH * A_H].set(sinks_like.reshape(-1))
    rows = [b_ada_like.reshape(DEPTH * 6, D), ln1_g, ln1_b, ln2_g, ln2_b, sink_row[None], loss_row[None],
            jnp.zeros((SMALL_ROWS - DEPTH * 10 - 2, D), F32)]
    return jnp.concatenate(rows)


def _unpack_small(p):
    n0 = DEPTH * 6
    return (p[:n0].reshape(DEPTH, 6 * D), p[n0:n0 + 4], p[n0 + 4:n0 + 8], p[n0 + 8:n0 + 12], p[n0 + 12:n0 + 16],
            p[n0 + 16, :DEPTH * A_H].reshape(DEPTH, A_H))


def kernel(x, c, w_ada, b_ada, w_in, sinks, w_a, w_b, w_o, ln1_g, ln1_b, w_gate, w_up, w_down, ln2_g, ln2_b, loss_target, m_w_ada, m_b_ada, m_w_in, m_sinks, m_w_a, m_w_b, m_w_o, m_ln1_g, m_ln1_b, m_w_gate, m_w_up, m_w_down, m_ln2_g, m_ln2_b, v_w_ada, v_b_ada, v_w_in, v_sinks, v_w_a, v_w_b, v_w_o, v_ln1_g, v_ln1_b, v_w_gate, v_w_up, v_w_down, v_ln2_g, v_ln2_b):
    s_len = x.shape[1]
    tm = 256
    xi, yi, ci = _my_pos()
    chip = 2 * xi + yi
    dev = 4 * xi + 2 * yi + ci
    n_ada = w_ada.shape[2]

    c_all = _all_gather_small(jnp.concatenate([c, jnp.zeros((7, D), F32)]))[:, 0]
    b_cols = lax.dynamic_slice_in_dim(b_ada, chip * n_ada, n_ada, axis=1)
    mod_cols = _ada_forward(c_all, w_ada, jnp.broadcast_to(b_cols[:, None, :], (DEPTH, N_DEV, n_ada)))
    mod_all = _all_gather_small(mod_cols.reshape(DEPTH * N_DEV, n_ada))
    mod_all = mod_all.reshape(N_CHIPS, 2, DEPTH, N_DEV, n_ada)[:, 0]
    mod_mine = lax.dynamic_index_in_dim(mod_all, dev, axis=2, keepdims=False)
    mod = jnp.transpose(mod_mine, (1, 0, 2)).reshape(DEPTH, N_CHIPS * n_ada)

    big = (w_in, w_a, w_b, w_o, w_gate, w_up, w_down)
    gathered = []
    for l in range(DEPTH):
        g_in, g_a, g_b, g_o, g_g, g_u, g_d = _gather_weights([w[l].astype(BF16) for w in big])
        gathered.append((g_in, g_a, g_b, g_o.reshape(D, D), g_g, g_u, g_d))

    loss_blk, grad_x, grads, dmod, dln, dsinks = _local_step(
        x[0], loss_target[0], mod, sinks, (ln1_g, ln1_b, ln2_g, ln2_b), gathered, tm)

    stacked = [jnp.stack([grads[l][w] for l in range(DEPTH)]) for w in range(len(big))]
    recv = _scatter_grads(stacked)
    part = [_sum_slots(r.reshape(N_CHIPS, -1, r.shape[-1])) for r in recv]
    other = _swap_with_sibling(part)
    big_m = (m_w_in, m_w_a, m_w_b, m_w_o, m_w_gate, m_w_up, m_w_down)
    big_v = (v_w_in, v_w_a, v_w_b, v_w_o, v_w_gate, v_w_up, v_w_down)
    big_out = []
    for w, mm, vv, p, q in zip(big, big_m, big_v, part, other):
        shp = w.shape
        flat = lambda t: t.reshape(-1, shp[-1])
        res = _adam(flat(w), [p, q], flat(mm), flat(vv))
        big_out.append([t.reshape(shp) for t in res])

    small = _pack_small(dmod.reshape(DEPTH, 6 * D), dln[:, 0], dln[:, 1], dln[:, 2], dln[:, 3], dsinks, loss_blk[0, :1].repeat(D))
    small_all = _all_gather_small(small)
    sg, sd, sm, sv = _small_reduce_adam(
        small_all,
        _pack_small(b_ada, ln1_g, ln1_b, ln2_g, ln2_b, sinks, jnp.zeros((D,), F32)),
        _pack_small(m_b_ada, m_ln1_g, m_ln1_b, m_ln2_g, m_ln2_b, m_sinks, jnp.zeros((D,), F32)),
        _pack_small(v_b_ada, v_ln1_g, v_ln1_b, v_ln2_g, v_ln2_b, v_sinks, jnp.zeros((D,), F32)))
    loss = sg[DEPTH * 10 + 1, 0]
    g_small, d_small, m_small, v_small = _unpack_small(sg), _unpack_small(sd), _unpack_small(sm), _unpack_small(sv)

    dmod_all = small_all[:, :DEPTH * 6].reshape(N_DEV, DEPTH, 6 * D)
    dmod_cols = lax.dynamic_slice_in_dim(dmod_all, chip * n_ada, n_ada, axis=2)
    dmod_pad = jnp.concatenate([jnp.transpose(dmod_cols, (1, 0, 2)), jnp.zeros((DEPTH, 8, n_ada), F32)], axis=1)
    c_pad = jnp.concatenate([c_all, jnp.zeros((8, D), F32)])
    g_ada = _ada_backward(c_pad, dmod_pad)
    flat_ada = lambda t: t.reshape(-1, n_ada)
    ada_out = [t.reshape(w_ada.shape) for t in _adam(flat_ada(w_ada), [flat_ada(g_ada)], flat_ada(m_w_ada), flat_ada(v_w_ada))]

    def ordered(k):
        sm_k = (g_small, d_small, m_small, v_small)[k]
        bg = [o[k] for o in big_out]
        return [ada_out[k], sm_k[0], bg[0], sm_k[5], bg[1], bg[2], bg[3], sm_k[1], sm_k[2], bg[4], bg[5], bg[6],
                sm_k[3], sm_k[4]]

    return (loss, grad_x[None], *ordered(0), *ordered(1), *ordered(2), *ordered(3))
```

```python
import functools

import numpy as np
import jax
import jax.numpy as jnp
from jax import lax
from jax.experimental import pallas as pl
from jax.experimental.pallas import tpu as pltpu

F32 = jnp.float32
BF16 = jnp.bfloat16
MESH = pl.DeviceIdType.MESH

D = 1024
DEPTH = 4
HD = 64
A_H = 8
A_HKV = 2
A_WINDOW = 128
B_GROUPS = ((128, 1), (512, 4), (2048, 16))
B_HG = 4
N_HEADS = A_H + B_HG * len(B_GROUPS)
BLK = 128
A_QW = A_H * HD
A_KW = A_HKV * HD
B_W = B_HG * len(B_GROUPS) * HD
B_OW = B_HG * HD
QKV_W = A_QW + 2 * A_KW + 3 * B_W
IN_W = QKV_W + 2 * D
D_FF = 2816
N_CHIPS = 4
N_DEV = 8
FF_S = D_FF // N_CHIPS
IN_S = IN_W // N_CHIPS
ALPHA = (2 * DEPTH) ** 0.25
LN_EPS = 1e-5
NEG_INF = -1e30
ADAM_LR = 0.001
ADAM_B1 = 0.9
ADAM_B2 = 0.999
ADAM_EPS = 1e-08
ADAM_WD = 0.01
ADAM_STEP = 10
SMALL_ROWS = 48
VMEM_LIMIT = 56 * 1024 * 1024


def _cp(sem):
    return pltpu.CompilerParams(dimension_semantics=sem, vmem_limit_bytes=VMEM_LIMIT)


def _nt(a, b):
    return lax.dot_general(a, b, (((1,), (1,)), ((), ())), preferred_element_type=F32)


def _tn(a, b):
    return lax.dot_general(a, b, (((0,), (0,)), ((), ())), preferred_element_type=F32)


def _nn(a, b):
    return jnp.dot(a, b, preferred_element_type=F32)


def _sigmoid(x):
    return jax.nn.sigmoid(x)


def _layer_norm_rows(r, g, b):
    mu = jnp.mean(r, axis=-1, keepdims=True)
    cen = r - mu
    var = jnp.mean(cen * cen, axis=-1, keepdims=True)
    rstd = lax.rsqrt(var + LN_EPS)
    xhat = cen * rstd
    return xhat, rstd, xhat * g + b


def _layer_norm_bwd_rows(dy, xhat, rstd, g):
    dxh = dy * g
    m1 = jnp.mean(dxh, axis=-1, keepdims=True)
    m2 = jnp.mean(dxh * xhat, axis=-1, keepdims=True)
    return rstd * (dxh - m1 - xhat * m2)


def _colsum(v):
    return jnp.sum(v, axis=0, keepdims=True)


def _my_pos():
    return lax.axis_index("x"), lax.axis_index("y"), lax.axis_index("c")


def _flip(v, bit):
    return 1 - v if bit else v


def _all_gather_small(v):
    rows, cols = v.shape

    def body(v_ref, o_ref, send_sems, recv_sems):
        x, y, c = _my_pos()
        me = 4 * x + 2 * y + c
        o_ref[me] = v_ref[...]
        peers = []
        for k in range(1, N_DEV):
            peers.append((_flip(x, k & 4), _flip(y, k & 2), _flip(c, k & 1)))
        sends = []
        for k, peer in enumerate(peers):
            cp = pltpu.make_async_remote_copy(
                src_ref=v_ref, dst_ref=o_ref.at[me], send_sem=send_sems.at[k], recv_sem=recv_sems.at[k],
                device_id=peer, device_id_type=MESH)
            cp.start()
            sends.append(cp)
        for k, (px, py, pc) in enumerate(peers):
            pltpu.make_async_remote_copy(
                src_ref=v_ref, dst_ref=o_ref.at[4 * px + 2 * py + pc], send_sem=send_sems.at[k],
                recv_sem=recv_sems.at[k], device_id=(px, py, pc), device_id_type=MESH).wait_recv()
        for cp in sends:
            cp.wait_send()

    return pl.pallas_call(
        body,
        out_shape=jax.ShapeDtypeStruct((N_DEV, rows, cols), F32),
        in_specs=[pl.BlockSpec(memory_space=pltpu.VMEM)],
        out_specs=pl.BlockSpec(memory_space=pltpu.VMEM),
        scratch_shapes=[pltpu.SemaphoreType.DMA((N_DEV - 1,)), pltpu.SemaphoreType.DMA((N_DEV - 1,))],
        name="all_gather_small",
    )(v)


def _chip_peers(x, y):
    return [(_flip(x, k & 2), _flip(y, k & 1)) for k in (1, 2, 3)]


def _gather_weights(shards):
    n = len(shards)

    def body(*refs):
        ins, outs = refs[:n], refs[n:2 * n]
        send_sems, recv_sems, loc_sems = refs[2 * n:]
        x, y, c = _my_pos()
        me = 2 * x + y
        local = []
        for w in range(n):
            cp = pltpu.make_async_copy(ins[w], outs[w].at[me], loc_sems.at[w])
            cp.start()
            local.append(cp)
        peers = _chip_peers(x, y)
        sends = []
        for k, (px, py) in enumerate(peers):
            for w in range(n):
                cp = pltpu.make_async_remote_copy(
                    src_ref=ins[w], dst_ref=outs[w].at[me], send_sem=send_sems.at[3 * w + k],
                    recv_sem=recv_sems.at[3 * w + k], device_id=(px, py, c), device_id_type=MESH)
                cp.start()
                sends.append(cp)
        for k, (px, py) in enumerate(peers):
            for w in range(n):
                pltpu.make_async_remote_copy(
                    src_ref=ins[w], dst_ref=outs[w].at[2 * px + py], send_sem=send_sems.at[3 * w + k],
                    recv_sem=recv_sems.at[3 * w + k], device_id=(px, py, c), device_id_type=MESH).wait_recv()
        for cp in sends:
            cp.wait_send()
        for cp in local:
            cp.wait()

    return pl.pallas_call(
        body,
        out_shape=[jax.ShapeDtypeStruct((N_CHIPS,) + s.shape, s.dtype) for s in shards],
        in_specs=[pl.BlockSpec(memory_space=pl.ANY)] * n,
        out_specs=[pl.BlockSpec(memory_space=pl.ANY)] * n,
        scratch_shapes=[pltpu.SemaphoreType.DMA((3 * n,)), pltpu.SemaphoreType.DMA((3 * n,)),
                        pltpu.SemaphoreType.DMA((n,))],
        name="gather_weights",
    )(*shards)


def _scatter_grads(grads):
    n = len(grads)

    def body(*refs):
        ins, outs = refs[:n], refs[n:2 * n]
        send_sems, recv_sems, loc_sems = refs[2 * n:]
        x, y, c = _my_pos()
        me = 2 * x + y
        local = []
        for w in range(n):
            cp = pltpu.make_async_copy(ins[w].at[:, me], outs[w].at[me], loc_sems.at[w])
            cp.start()
            local.append(cp)
        peers = _chip_peers(x, y)
        sends = []
        for k, (px, py) in enumerate(peers):
            for w in range(n):
                cp = pltpu.make_async_remote_copy(
                    src_ref=ins[w].at[:, 2 * px + py], dst_ref=outs[w].at[me], send_sem=send_sems.at[3 * w + k],
                    recv_sem=recv_sems.at[3 * w + k], device_id=(px, py, c), device_id_type=MESH)
                cp.start()
                sends.append(cp)
        for k, (px, py) in enumerate(peers):
            for w in range(n):
                pltpu.make_async_remote_copy(
                    src_ref=ins[w].at[:, me], dst_ref=outs[w].at[2 * px + py], send_sem=send_sems.at[3 * w + k],
                    recv_sem=recv_sems.at[3 * w + k], device_id=(px, py, c), device_id_type=MESH).wait_recv()
        for cp in sends:
            cp.wait_send()
        for cp in local:
            cp.wait()

    outs = []
    for g in grads:
        nl, nj, k, m = g.shape
        outs.append(jax.ShapeDtypeStruct((nj, nl, k, m), g.dtype))
    return pl.pallas_call(
        body,
        out_shape=outs,
        in_specs=[pl.BlockSpec(memory_space=pl.ANY)] * n,
        out_specs=[pl.BlockSpec(memory_space=pl.ANY)] * n,
        scratch_shapes=[pltpu.SemaphoreType.DMA((3 * n,)), pltpu.SemaphoreType.DMA((3 * n,)),
                        pltpu.SemaphoreType.DMA((n,))],
        name="scatter_grads",
    )(*grads)


def _swap_with_sibling(parts):
    n = len(parts)

    def body(*refs):
        ins, outs = refs[:n], refs[n:2 * n]
        send_sems, recv_sems = refs[2 * n:]
        x, y, c = _my_pos()
        sib = (x, y, 1 - c)
        cps = []
        for w in range(n):
            cp = pltpu.make_async_remote_copy(
                src_ref=ins[w], dst_ref=outs[w], send_sem=send_sems.at[w], recv_sem=recv_sems.at[w],
                device_id=sib, device_id_type=MESH)
            cp.start()
            cps.append(cp)
        for cp in cps:
            cp.wait_recv()
        for cp in cps:
            cp.wait_send()

    return pl.pallas_call(
        body,
        out_shape=[jax.ShapeDtypeStruct(p.shape, p.dtype) for p in parts],
        in_specs=[pl.BlockSpec(memory_space=pl.ANY)] * n,
        out_specs=[pl.BlockSpec(memory_space=pl.ANY)] * n,
        scratch_shapes=[pltpu.SemaphoreType.DMA((n,)), pltpu.SemaphoreType.DMA((n,))],
        name="swap_with_sibling",
    )(*parts)


def _row_tile(rows, cols, target_elems=512 * 1024):
    t = max(8, min(rows, (target_elems // cols) // 8 * 8))
    while rows % t:
        t -= 8
    return t


def _sum_slots(recv):
    _, rows, cols = recv.shape
    tr = _row_tile(rows, cols)

    def body(r_ref, o_ref):
        acc = r_ref[0].astype(F32) + r_ref[1].astype(F32)
        acc = acc + r_ref[2].astype(F32)
        o_ref[...] = acc + r_ref[3].astype(F32)

    return pl.pallas_call(
        body, grid=(rows // tr,),
        in_specs=[pl.BlockSpec((N_CHIPS, tr, cols), lambda i: (0, i, 0))],
        out_specs=pl.BlockSpec((tr, cols), lambda i: (i, 0)),
        out_shape=jax.ShapeDtypeStruct((rows, cols), F32),
        compiler_params=_cp(("parallel",)), name="sum_slots",
    )(recv)


def _adam_math(w, g, m, v):
    m = ADAM_B1 * m + (1.0 - ADAM_B1) * g
    v = ADAM_B2 * v + (1.0 - ADAM_B2) * (g * g)
    m_hat = m / (1.0 - ADAM_B1 ** ADAM_STEP)
    v_hat = v / (1.0 - ADAM_B2 ** ADAM_STEP)
    delta = -ADAM_LR * (m_hat / (jnp.sqrt(v_hat) + ADAM_EPS) + ADAM_WD * w)
    return delta, m, v


def _adam(w, g_parts, m, v):
    rows, cols = w.shape
    tr = _row_tile(rows, cols, 128 * 1024)
    ng = len(g_parts)

    def body(*refs):
        w_ref, m_ref, v_ref = refs[0], refs[1], refs[2]
        g_refs = refs[3:3 + ng]
        g_out, d_out, m_out, v_out = refs[3 + ng:]
        g = g_refs[0][...]
        for r in g_refs[1:]:
            g = g + r[...]
        delta, nm, nv = _adam_math(w_ref[...], g, m_ref[...], v_ref[...])
        g_out[...] = g
        d_out[...] = delta
        m_out[...] = nm
        v_out[...] = nv

    spec = pl.BlockSpec((tr, cols), lambda i: (i, 0))
    shp = jax.ShapeDtypeStruct((rows, cols), F32)
    return pl.pallas_call(
        body, grid=(rows // tr,),
        in_specs=[spec] * (3 + ng), out_specs=[spec] * 4, out_shape=[shp] * 4,
        compiler_params=_cp(("parallel",)), name="adam",
    )(w, m, v, *g_parts)


def _small_reduce_adam(gathered, w, m, v):
    _, rows, cols = gathered.shape

    def body(g_ref, w_ref, m_ref, v_ref, g_out, d_out, m_out, v_out):
        g = g_ref[0]
        for k in range(1, N_DEV):
            g = g + g_ref[k]
        delta, nm, nv = _adam_math(w_ref[...], g, m_ref[...], v_ref[...])
        g_out[...] = g
        d_out[...] = delta
        m_out[...] = nm
        v_out[...] = nv

    shp = jax.ShapeDtypeStruct((rows, cols), F32)
    return pl.pallas_call(body, out_shape=[shp] * 4, name="small_reduce_adam")(gathered, w, m, v)


def _modulate(x, modv, row_s, row_sh, tm):
    s_len, dm = x.shape

    def body(x_ref, mod_ref, u_ref):
        s = mod_ref[row_s:row_s + 1, :]
        sh = mod_ref[row_sh:row_sh + 1, :]
        u_ref[...] = (x_ref[...] * (1.0 + s) + sh).astype(BF16)

    return pl.pallas_call(
        body, grid=(s_len // tm,),
        in_specs=[pl.BlockSpec((tm, dm), lambda i: (i, 0)), pl.BlockSpec((8, dm), lambda i: (0, 0))],
        out_specs=pl.BlockSpec((tm, dm), lambda i: (i, 0)),
        out_shape=jax.ShapeDtypeStruct((s_len, dm), BF16),
        compiler_params=_cp(("parallel",)), name="modulate",
    )(x, modv)


def _loss_head(y, target, tm):
    s_len, dm = y.shape

    def body(y_ref, t_ref, dy_ref, l_ref):
        i = pl.program_id(0)

        @pl.when(i == 0)
        def _():
            l_ref[...] = jnp.zeros_like(l_ref)

        diff = y_ref[...] - t_ref[...]
        dy_ref[...] = diff / dm
        per_tok = jnp.mean(diff * diff, axis=-1, keepdims=True)
        l_ref[...] += 0.5 * jnp.sum(per_tok, axis=0, keepdims=True)

    return pl.pallas_call(
        body, grid=(s_len // tm,),
        in_specs=[pl.BlockSpec((tm, dm), lambda i: (i, 0))] * 2,
        out_specs=[pl.BlockSpec((tm, dm), lambda i: (i, 0)), pl.BlockSpec((8, 128), lambda i: (0, 0))],
        out_shape=[jax.ShapeDtypeStruct((s_len, dm), F32), jax.ShapeDtypeStruct((8, 128), F32)],
        compiler_params=_cp(("arbitrary",)), name="loss_head",
    )(y, target)


def _ada_forward(c_all, w_ada, b_cols):
    nl, dm, n = w_ada.shape

    def body(c_ref, w_ref, b_ref, o_ref):
        cv = c_ref[...]
        sc = (cv * _sigmoid(cv)).astype(BF16)
        o_ref[...] = _nn(sc, w_ref[...].astype(BF16)) + b_ref[...]

    return pl.pallas_call(
        body, grid=(nl,),
        in_specs=[pl.BlockSpec((N_DEV, dm), lambda l: (0, 0)),
                  pl.BlockSpec((None, dm, n), lambda l: (l, 0, 0)),
                  pl.BlockSpec((None, N_DEV, n), lambda l: (l, 0, 0))],
        out_specs=pl.BlockSpec((None, N_DEV, n), lambda l: (l, 0, 0)),
        out_shape=jax.ShapeDtypeStruct((nl, N_DEV, n), F32),
        compiler_params=_cp(("parallel",)), name="ada_forward",
    )(c_all, w_ada, b_cols)


def _ada_backward(c_pad, dmod_pad):
    nl, npad, n = dmod_pad.shape
    dm = c_pad.shape[1]

    def body(c_ref, d_ref, o_ref):
        cv = c_ref[...]
        sc = (cv * _sigmoid(cv)).astype(BF16)
        o_ref[...] = _tn(sc, d_ref[...].astype(BF16))

    return pl.pallas_call(
        body, grid=(nl,),
        in_specs=[pl.BlockSpec((npad, dm), lambda l: (0, 0)),
                  pl.BlockSpec((None, npad, n), lambda l: (l, 0, 0))],
        out_specs=pl.BlockSpec((None, dm, n), lambda l: (l, 0, 0)),
        out_shape=jax.ShapeDtypeStruct((nl, dm, n), F32),
        compiler_params=_cp(("parallel",)), name="ada_backward",
    )(c_pad, dmod_pad)


def _in_proj(u, w3, t_lo, n_t, tn, out_dtype, tm, name):
    s_len, kd = u.shape
    n = w3.shape[2]
    tps = n // tn

    def body(u_ref, w_ref, o_ref):
        o_ref[...] = _nn(u_ref[...], w_ref[...]).astype(out_dtype)

    return pl.pallas_call(
        body, grid=(s_len // tm, n_t),
        in_specs=[pl.BlockSpec((tm, kd), lambda i, t: (i, 0)),
                  pl.BlockSpec((None, kd, tn), lambda i, t: ((t + t_lo) // tps, 0, (t + t_lo) % tps))],
        out_specs=pl.BlockSpec((tm, tn), lambda i, t: (i, t)),
        out_shape=jax.ShapeDtypeStruct((s_len, n_t * tn), out_dtype),
        compiler_params=_cp(("parallel", "arbitrary")), name=name,
    )(u, w3)


def _mix_out(ya, o_g, lse_g, gates, x_in, modv, lnv, w_a, w_b, w_o, tm):
    s_len = ya.shape[0]

    def body(ya_ref, o0, o1, o2, l0, l1, l2, g_ref, x_ref, mod_ref, ln_ref, wa_ref, wb_ref, wo_ref,
             xm_ref, xhat_ref, rstd_ref, u2_ref, t1_ref, yb_ref):
        la, lb, lc = l0[...], l1[...], l2[...]
        mx = jnp.maximum(jnp.maximum(la, lb), lc)
        ea, eb, ec = jnp.exp(la - mx), jnp.exp(lb - mx), jnp.exp(lc - mx)
        den = ea + eb + ec
        yb = (o0[...] * (ea / den) + o1[...] * (eb / den) + o2[...] * (ec / den)).astype(BF16)
        yb_ref[...] = yb
        yav = ya_ref[...]
        za = jnp.concatenate([_nn(yav, wa_ref[j]) for j in range(N_CHIPS)], axis=1)
        zb = jnp.concatenate([_nn(yb, wb_ref[j]) for j in range(N_CHIPS)], axis=1)
        merged = _sigmoid(g_ref[:, 0:D]) * za + _sigmoid(g_ref[:, D:2 * D]) * zb
        t1 = _nn(merged.astype(BF16), wo_ref[...])
        t1_ref[...] = t1.astype(BF16)
        r = ALPHA * x_ref[...] + mod_ref[2:3, :] * t1
        xhat, rstd, xm = _layer_norm_rows(r, ln_ref[0:1, :], ln_ref[1:2, :])
        xhat_ref[...] = xhat
        rstd_ref[...] = jnp.broadcast_to(rstd, (tm, 128))
        xm_ref[...] = xm
        u2_ref[...] = (xm * (1.0 + mod_ref[4:5, :]) + mod_ref[3:4, :]).astype(BF16)

    tok = lambda w: pl.BlockSpec((tm, w), lambda i: (i, 0))
    full = lambda shp: pl.BlockSpec(shp, lambda i: (0,) * len(shp))
    return pl.pallas_call(
        body, grid=(s_len // tm,),
        in_specs=[tok(A_QW)] + [tok(B_OW)] * 6 + [tok(2 * D), tok(D), full((8, D)), full((8, D)),
                  full(w_a.shape), full(w_b.shape), full(w_o.shape)],
        out_specs=[tok(D), tok(D), tok(128), tok(D), tok(D), tok(B_OW)],
        out_shape=[jax.ShapeDtypeStruct((s_len, D), F32), jax.ShapeDtypeStruct((s_len, D), F32),
                   jax.ShapeDtypeStruct((s_len, 128), F32), jax.ShapeDtypeStruct((s_len, D), BF16),
                   jax.ShapeDtypeStruct((s_len, D), BF16), jax.ShapeDtypeStruct((s_len, B_OW), BF16)],
        compiler_params=_cp(("parallel",)), name="mix_out",
    )(ya, *o_g, *lse_g, gates, x_in, modv, lnv, w_a, w_b, w_o)


def _ffn_forward(u2, x_mid, modv, lnv, mod_next, w_g, w_u, w_d, tm):
    s_len = u2.shape[0]

    def body(u_ref, xm_ref, mod_ref, ln_ref, nxt_ref, wg_ref, wu_ref, wd_ref,
             a_ref, b_ref, t2_ref, xhat_ref, rstd_ref, xo_ref, un_ref, acc_ref):
        j = pl.program_id(1)

        @pl.when(j == 0)
        def _():
            acc_ref[...] = jnp.zeros_like(acc_ref)

        uv = u_ref[...]
        a = _nn(uv, wg_ref[...])
        b = _nn(uv, wu_ref[...])
        a_ref[...] = a
        b_ref[...] = b
        f = (a * _sigmoid(a)) * b
        acc_ref[...] += _nn(f.astype(BF16), wd_ref[...])

        @pl.when(j == N_CHIPS - 1)
        def _():
            t2 = acc_ref[...]
            t2_ref[...] = t2.astype(BF16)
            r = ALPHA * xm_ref[...] + mod_ref[5:6, :] * t2
            xhat, rstd, xo = _layer_norm_rows(r, ln_ref[2:3, :], ln_ref[3:4, :])
            xhat_ref[...] = xhat
            rstd_ref[...] = jnp.broadcast_to(rstd, (tm, 128))
            xo_ref[...] = xo
            un_ref[...] = (xo * (1.0 + nxt_ref[1:2, :]) + nxt_ref[0:1, :]).astype(BF16)

    tok = lambda w: pl.BlockSpec((tm, w), lambda i, j: (i, 0))
    full = lambda shp: pl.BlockSpec(shp, lambda i, j: (0,) * len(shp))
    hid = pl.BlockSpec((None, tm, FF_S), lambda i, j: (j, i, 0))
    return pl.pallas_call(
        body, grid=(s_len // tm, N_CHIPS),
        in_specs=[tok(D), tok(D), full((8, D)), full((8, D)), full((8, D)),
                  pl.BlockSpec((None, D, FF_S), lambda i, j: (j, 0, 0)),
                  pl.BlockSpec((None, D, FF_S), lambda i, j: (j, 0, 0)),
                  pl.BlockSpec((None, FF_S, D), lambda i, j: (j, 0, 0))],
        out_specs=[hid, hid, tok(D), tok(D), tok(128), tok(D), tok(D)],
        out_shape=[jax.ShapeDtypeStruct((N_CHIPS, s_len, FF_S), F32),
                   jax.ShapeDtypeStruct((N_CHIPS, s_len, FF_S), F32),
                   jax.ShapeDtypeStruct((s_len, D), BF16), jax.ShapeDtypeStruct((s_len, D), F32),
                   jax.ShapeDtypeStruct((s_len, 128), F32), jax.ShapeDtypeStruct((s_len, D), F32),
                   jax.ShapeDtypeStruct((s_len, D), BF16)],
        scratch_shapes=[pltpu.VMEM((tm, D), F32)],
        compiler_params=_cp(("parallel", "arbitrary")), name="ffn_forward",
    )(u2, x_mid, modv, lnv, mod_next, w_g, w_u, w_d)


def _ffn_backward(dxo, xhat2, rstd2, t2, x_mid, a, b, modv, lnv, w_g, w_u, w_d, tm):
    s_len = dxo.shape[0]

    def body(dxo_ref, xhat_ref, rstd_ref, t2_ref, xm_ref, a_ref, b_ref, mod_ref, ln_ref, wg_ref, wu_ref, wd_ref,
             dxm_ref, dt2_ref, da_ref, db_ref, f_ref, small_ref, dr_s, dt_s, acc_ref):
        i, j = pl.program_id(0), pl.program_id(1)

        @pl.when((i == 0) & (j == 0))
        def _():
            small_ref[...] = jnp.zeros_like(small_ref)

        @pl.when(j == 0)
        def _():
            dxov, xhat = dxo_ref[...], xhat_ref[...]
            dr = _layer_norm_bwd_rows(dxov, xhat, rstd_ref[:, 0:1], ln_ref[2:3, :])
            small_ref[0:1, :] += _colsum(dxov * xhat)
            small_ref[1:2, :] += _colsum(dxov)
            small_ref[2:3, :] += _colsum(dr * t2_ref[...].astype(F32))
            dt = (mod_ref[5:6, :] * dr).astype(BF16)
            dt2_ref[...] = dt
            dt_s[...] = dt
            dr_s[...] = dr
            acc_ref[...] = jnp.zeros_like(acc_ref)

        df = _nt(dt_s[...], wd_ref[...])
        av, bv = a_ref[...], b_ref[...]
        sa = _sigmoid(av)
        sl = av * sa
        da = (df * bv * (sa * (1.0 + av * (1.0 - sa)))).astype(BF16)
        db = (df * sl).astype(BF16)
        da_ref[...] = da
        db_ref[...] = db
        f_ref[...] = (sl * bv).astype(BF16)
        acc_ref[...] += _nt(da, wg_ref[...]) + _nt(db, wu_ref[...])

        @pl.when(j == N_CHIPS - 1)
        def _():
            du = acc_ref[...]
            dxm_ref[...] = ALPHA * dr_s[...] + du * (1.0 + mod_ref[4:5, :])
            small_ref[3:4, :] += _colsum(du * xm_ref[...])
            small_ref[4:5, :] += _colsum(du)

    tok = lambda w: pl.BlockSpec((tm, w), lambda i, j: (i, 0))
    full = lambda shp: pl.BlockSpec(shp, lambda i, j: (0,) * len(shp))
    hid = pl.BlockSpec((None, tm, FF_S), lambda i, j: (j, i, 0))
    hid_bf = jax.ShapeDtypeStruct((N_CHIPS, s_len, FF_S), BF16)
    return pl.pallas_call(
        body, grid=(s_len // tm, N_CHIPS),
        in_specs=[tok(D), tok(D), tok(128), tok(D), tok(D), hid, hid, full((8, D)), full((8, D)),
                  pl.BlockSpec((None, D, FF_S), lambda i, j: (j, 0, 0)),
                  pl.BlockSpec((None, D, FF_S), lambda i, j: (j, 0, 0)),
                  pl.BlockSpec((None, FF_S, D), lambda i, j: (j, 0, 0))],
        out_specs=[tok(D), tok(D), hid, hid, hid, full((8, D))],
        out_shape=[jax.ShapeDtypeStruct((s_len, D), F32), jax.ShapeDtypeStruct((s_len, D), BF16),
                   hid_bf, hid_bf, hid_bf, jax.ShapeDtypeStruct((8, D), F32)],
        scratch_shapes=[pltpu.VMEM((tm, D), F32), pltpu.VMEM((tm, D), BF16), pltpu.VMEM((tm, D), F32)],
        compiler_params=_cp(("arbitrary", "arbitrary")), name="ffn_backward",
    )(dxo, xhat2, rstd2, t2, x_mid, a, b, modv, lnv, w_g, w_u, w_d)


def _mix_backward(dxm, xhat1, rstd1, t1, gates, ya, yb, modv, lnv, w_a, w_b, w_o, tm):
    s_len = dxm.shape[0]

    def body(dxm_ref, xhat_ref, rstd_ref, t1_ref, g_ref, ya_ref, yb_ref, mod_ref, ln_ref, wa_ref, wb_ref, wo_ref,
             dxi_ref, dt1_ref, mg_ref, dza_ref, dzb_ref, dg_ref, dya_ref, dyb_ref, small_ref):
        i = pl.program_id(0)

        @pl.when(i == 0)
        def _():
            small_ref[...] = jnp.zeros_like(small_ref)

        dxmv, xhat = dxm_ref[...], xhat_ref[...]
        dr = _layer_norm_bwd_rows(dxmv, xhat, rstd_ref[:, 0:1], ln_ref[0:1, :])
        small_ref[0:1, :] += _colsum(dxmv * xhat)
        small_ref[1:2, :] += _colsum(dxmv)
        small_ref[2:3, :] += _colsum(dr * t1_ref[...].astype(F32))
        dxi_ref[...] = ALPHA * dr
        dt1 = (mod_ref[2:3, :] * dr).astype(BF16)
        dt1_ref[...] = dt1
        dmg = _nt(dt1, wo_ref[...])
        yav, ybv = ya_ref[...], yb_ref[...]
        za = jnp.concatenate([_nn(yav, wa_ref[j]) for j in range(N_CHIPS)], axis=1)
        zb = jnp.concatenate([_nn(ybv, wb_ref[j]) for j in range(N_CHIPS)], axis=1)
        sga, sgb = _sigmoid(g_ref[:, 0:D]), _sigmoid(g_ref[:, D:2 * D])
        mg_ref[...] = (sga * za + sgb * zb).astype(BF16)
        dza = (dmg * sga).astype(BF16)
        dzb = (dmg * sgb).astype(BF16)
        dza_ref[...] = dza
        dzb_ref[...] = dzb
        dg_ref[:, 0:D] = (dmg * za * (sga * (1.0 - sga))).astype(BF16)
        dg_ref[:, D:2 * D] = (dmg * zb * (sgb * (1.0 - sgb))).astype(BF16)
        cw = D // N_CHIPS
        dya = _nt(dza[:, 0:cw], wa_ref[0])
        dyb = _nt(dzb[:, 0:cw], wb_ref[0])
        for j in range(1, N_CHIPS):
            dya = dya + _nt(dza[:, j * cw:(j + 1) * cw], wa_ref[j])
            dyb = dyb + _nt(dzb[:, j * cw:(j + 1) * cw], wb_ref[j])
        dya_ref[...] = dya.astype(BF16)
        dyb_ref[...] = dyb

    tok = lambda w: pl.BlockSpec((tm, w), lambda i: (i, 0))
    full = lambda shp: pl.BlockSpec(shp, lambda i: (0,) * len(shp))
    sd = lambda w, dt: jax.ShapeDtypeStruct((s_len, w), dt)
    return pl.pallas_call(
        body, grid=(s_len // tm,),
        in_specs=[tok(D), tok(D), tok(128), tok(D), tok(2 * D), tok(A_QW), tok(B_OW), full((8, D)), full((8, D)),
                  full(w_a.shape), full(w_b.shape), full(w_o.shape)],
        out_specs=[tok(D), tok(D), tok(D), tok(D), tok(D), tok(2 * D), tok(A_QW), tok(B_OW), full((8, D))],
        out_shape=[sd(D, F32), sd(D, BF16), sd(D, BF16), sd(D, BF16), sd(D, BF16), sd(2 * D, BF16),
                   sd(A_QW, BF16), sd(B_OW, F32), jax.ShapeDtypeStruct((8, D), F32)],
        compiler_params=_cp(("arbitrary",)), name="mix_backward",
    )(dxm, xhat1, rstd1, t1, gates, ya, yb, modv, lnv, w_a, w_b, w_o)


def _split3(v):
    hi = v.astype(BF16)
    r1 = v - hi.astype(F32)
    mid = r1.astype(BF16)
    lo = (r1 - mid.astype(F32)).astype(BF16)
    return hi, mid, lo


def _group_mix_backward(dyb, o_g, lse_g, tm):
    s_len = dyb.shape[0]

    def body(dyb_ref, o0, o1, o2, l0, l1, l2, do0, do1, do2, dl0, dl1, dl2):
        rr = lax.shift_right_logical(lax.broadcasted_iota(jnp.int32, (B_OW, B_OW), 0), 6)
        cc = lax.shift_right_logical(lax.broadcasted_iota(jnp.int32, (B_OW, B_OW), 1), 6)
        ones_bd = jnp.where(rr == cc, 1.0, 0.0).astype(BF16)

        def head_sum(v):
            hi, mid, lo = _split3(v)
            return _nn(hi, ones_bd) + _nn(mid, ones_bd) + _nn(lo, ones_bd)

        la, lb, lc = l0[...], l1[...], l2[...]
        mx = jnp.maximum(jnp.maximum(la, lb), lc)
        ea, eb, ec = jnp.exp(la - mx), jnp.exp(lb - mx), jnp.exp(lc - mx)
        den = ea + eb + ec
        wts = (ea / den, eb / den, ec / den)
        dy = dyb_ref[...]
        dws = [head_sum(dy * o[...]) for o in (o0, o1, o2)]
        dot = wts[0] * dws[0] + wts[1] * dws[1] + wts[2] * dws[2]
        for wt, dw, do_ref, dl_ref in zip(wts, dws, (do0, do1, do2), (dl0, dl1, dl2)):
            do_ref[...] = (dy * wt).astype(BF16)
            dl_ref[...] = wt * (dw - dot)

    tok = pl.BlockSpec((tm, B_OW), lambda i: (i, 0))
    return pl.pallas_call(
        body, grid=(s_len // tm,),
        in_specs=[tok] * 7, out_specs=[tok] * 6,
        out_shape=[jax.ShapeDtypeStruct((s_len, B_OW), BF16)] * 3 + [jax.ShapeDtypeStruct((s_len, B_OW), F32)] * 3,
        compiler_params=_cp(("parallel",)), name="group_mix_backward",
    )(dyb, *o_g, *lse_g)


def _in_proj_backward(dh, w_in, dxi, x_in, modv, tm):
    s_len = dh.shape[0]

    def body(dh_ref, w_ref, dxi_ref, x_ref, mod_ref, dx_ref, small_ref, acc_ref):
        i, j = pl.program_id(0), pl.program_id(1)

        @pl.when((i == 0) & (j == 0))
        def _():
            small_ref[...] = jnp.zeros_like(small_ref)

        @pl.when(j == 0)
        def _():
            acc_ref[...] = jnp.zeros_like(acc_ref)

        acc_ref[...] += _nt(dh_ref[...], w_ref[...])

        @pl.when(j == N_CHIPS - 1)
        def _():
            du = acc_ref[...]
            dx_ref[...] = dxi_ref[...] + du * (1.0 + mod_ref[1:2, :])
            small_ref[0:1, :] += _colsum(du * x_ref[...])
            small_ref[1:2, :] += _colsum(du)

    tok = lambda w: pl.BlockSpec((tm, w), lambda i, j: (i, 0))
    return pl.pallas_call(
        body, grid=(s_len // tm, N_CHIPS),
        in_specs=[pl.BlockSpec((tm, IN_S), lambda i, j: (i, j)),
                  pl.BlockSpec((None, D, IN_S), lambda i, j: (j, 0, 0)),
                  tok(D), tok(D), pl.BlockSpec((8, D), lambda i, j: (0, 0))],
        out_specs=[tok(D), pl.BlockSpec((8, D), lambda i, j: (0, 0))],
        out_shape=[jax.ShapeDtypeStruct((s_len, D), F32), jax.ShapeDtypeStruct((8, D), F32)],
        scratch_shapes=[pltpu.VMEM((tm, D), F32)],
        compiler_params=_cp(("arbitrary", "arbitrary")), name="in_proj_backward",
    )(dh, w_in, dxi, x_in, modv)


def _weight_grad(a, b, *, a_block, a_map, b_block, b_map, out_shape, out_block, out_map, n_panels, s_len, ts, name):
    acc_shape = tuple(d for d in out_block if d is not None)

    def body(a_ref, b_ref, o_ref, acc_ref):
        s = pl.program_id(1)

        @pl.when(s == 0)
        def _():
            acc_ref[...] = jnp.zeros_like(acc_ref)

        acc_ref[...] += _tn(a_ref[...], b_ref[...])

        @pl.when(s == s_len // ts - 1)
        def _():
            o_ref[...] = acc_ref[...].astype(BF16)

    return pl.pallas_call(
        body, grid=(n_panels, s_len // ts),
        in_specs=[pl.BlockSpec(a_block, a_map), pl.BlockSpec(b_block, b_map)],
        out_specs=pl.BlockSpec(out_block, out_map),
        out_shape=jax.ShapeDtypeStruct(out_shape, BF16),
        scratch_shapes=[pltpu.VMEM(acc_shape, F32)],
        compiler_params=_cp(("parallel", "arbitrary")), name=name,
    )(a, b)


def _bias_tables(slopes, max_dist, stride):
    qi = np.arange(BLK)[:, None]
    sj = np.arange(2 * BLK)[None, :]
    dist = qi + BLK - sj
    valid = (dist >= 0) & (dist <= max_dist)
    bias = -(jnp.asarray(slopes, F32).reshape(-1, 1, 1) * jnp.asarray(dist * stride, F32))
    gen = jnp.where(valid[None], bias, NEG_INF)
    first = jnp.where((valid & (sj >= BLK))[None], bias, NEG_INF)
    return jnp.stack([gen, first]).astype(F32)


def _attn_specs(dil, n_heads, n_kv, q_off, k_off, v_off, nb, total_w):
    qw, kw = n_heads * HD, n_kv * HD
    qs, ks = total_w // qw, total_w // kw
    q_spec = pl.BlockSpec((BLK, qw), lambda r, n: (jnp.minimum(n, nb - 1), r * qs + q_off // qw))
    kp = pl.BlockSpec((BLK, kw), lambda r, n: (jnp.maximum(n - 1, 0), r * ks + k_off // kw))
    kc = pl.BlockSpec((BLK, kw), lambda r, n: (jnp.minimum(n, nb - 1), r * ks + k_off // kw))
    vp = pl.BlockSpec((BLK, kw), lambda r, n: (jnp.maximum(n - 1, 0), r * ks + v_off // kw))
    vc = pl.BlockSpec((BLK, kw), lambda r, n: (jnp.minimum(n, nb - 1), r * ks + v_off // kw))
    return q_spec, kp, kc, vp, vc


def _attn_forward(qkv_f, bias, sinks, *, dil, n_heads, n_kv, q_off, k_off, v_off, out_dtype, name):
    seq, tw = qkv_f.shape
    total_w = tw // dil
    nb = seq // BLK
    grp = n_heads // n_kv
    qw = n_heads * HD
    has_sink = sinks is not None

    def body(*refs):
        if has_sink:
            sink_ref, refs = refs[0], refs[1:]
        q_ref, kp_ref, kc_ref, vp_ref, vc_ref, bias_ref, o_ref, lse_ref = refs
        n = pl.program_id(1)
        sel = jnp.where(n == 0, 1, 0)
        for h in range(n_heads):
            hk = h // grp
            hs, ks = slice(h * HD, (h + 1) * HD), slice(hk * HD, (hk + 1) * HD)
            qh = q_ref[:, hs] * 0.125
            sp = _nt(qh, kp_ref[:, ks]) + bias_ref[sel, h, :, 0:BLK]
            sc = _nt(qh, kc_ref[:, ks]) + bias_ref[sel, h, :, BLK:2 * BLK]
            m = jnp.maximum(jnp.max(sp, axis=-1, keepdims=True), jnp.max(sc, axis=-1, keepdims=True))
            if has_sink:
                m = jnp.maximum(m, sink_ref[0, h])
            ep, ec = jnp.exp(sp - m), jnp.exp(sc - m)
            den = jnp.sum(ep, axis=-1, keepdims=True) + jnp.sum(ec, axis=-1, keepdims=True)
            if has_sink:
                den = den + jnp.exp(sink_ref[0, h] - m)
            pv = _nn(ep.astype(BF16), vp_ref[:, ks]) + _nn(ec.astype(BF16), vc_ref[:, ks])
            o_ref[:, hs] = (pv * (1.0 / den)).astype(out_dtype)
            lse_ref[:, hs] = jnp.broadcast_to(m + jnp.log(den), (BLK, HD))

    q_spec, kp, kc, vp, vc = _attn_specs(dil, n_heads, n_kv, q_off, k_off, v_off, nb, total_w)
    o_spec = pl.BlockSpec((BLK, qw), lambda r, n: (n, r))
    in_specs = [q_spec, kp, kc, vp, vc, pl.BlockSpec(bias.shape, lambda r, n: (0, 0, 0, 0))]
    args = [qkv_f, qkv_f, qkv_f, qkv_f, qkv_f, bias]
    if has_sink:
        in_specs = [pl.BlockSpec(memory_space=pltpu.SMEM)] + in_specs
        args = [sinks] + args
    return pl.pallas_call(
        body, grid=(dil, nb), in_specs=in_specs, out_specs=[o_spec, o_spec],
        out_shape=[jax.ShapeDtypeStruct((seq, dil * qw), out_dtype), jax.ShapeDtypeStruct((seq, dil * qw), F32)],
        compiler_params=_cp(("parallel", "arbitrary")), name=name,
    )(*args)


def _attn_backward(qkv_f, do_f, lse_f, dlse_f, bias, sinks, *, dil, n_heads, n_kv, q_off, k_off, v_off, name):
    seq, tw = qkv_f.shape
    total_w = tw // dil
    nb = seq // BLK
    grp = n_heads // n_kv
    qw, kw = n_heads * HD, n_kv * HD
    has_sink = sinks is not None
    has_dlse = dlse_f is not None

    def body(*refs):
        refs = list(refs)
        sink_ref = refs.pop(0) if has_sink else None
        q_ref, kp_ref, kc_ref, vp_ref, vc_ref, do_ref, lse_ref = refs[:7]
        refs = refs[7:]
        dlse_ref = refs.pop(0) if has_dlse else None
        bias_ref, dq_ref, dk_ref, dv_ref, ds_ref, ck_ref, cv_ref = refs
        r, n = pl.program_id(0), pl.program_id(1)

        @pl.when((r == 0) & (n == 0))
        def _():
            ds_ref[...] = jnp.zeros_like(ds_ref)

        @pl.when(n == 0)
        def _():
            ck_ref[...] = jnp.zeros_like(ck_ref)
            cv_ref[...] = jnp.zeros_like(cv_ref)

        @pl.when(n < nb)
        def _():
            sel = jnp.where(n == 0, 1, 0)
            for hk in range(n_kv):
                ks = slice(hk * HD, (hk + 1) * HD)
                kpv, kcv, vpv, vcv = kp_ref[:, ks], kc_ref[:, ks], vp_ref[:, ks], vc_ref[:, ks]
                dkp = jnp.zeros((BLK, HD), F32)
                dkc = jnp.zeros((BLK, HD), F32)
                dvp = jnp.zeros((BLK, HD), F32)
                dvc = jnp.zeros((BLK, HD), F32)
                for h in range(hk * grp, (hk + 1) * grp):
                    hs = slice(h * HD, (h + 1) * HD)
                    qh = q_ref[:, hs] * 0.125
                    doh = do_ref[:, hs]
                    lse = lse_ref[:, h * HD:h * HD + 1]
                    pp = jnp.exp(_nt(qh, kpv) + bias_ref[sel, h, :, 0:BLK] - lse)
                    pc = jnp.exp(_nt(qh, kcv) + bias_ref[sel, h, :, BLK:2 * BLK] - lse)
                    dpp, dpc = _nt(doh, vpv), _nt(doh, vcv)
                    delta = (jnp.sum(pp * dpp, axis=-1, keepdims=True)
                             + jnp.sum(pc * dpc, axis=-1, keepdims=True))
                    shift = delta
                    if has_dlse:
                        shift = delta - dlse_ref[:, h * HD:h * HD + 1]
                    dsp = (pp * (dpp - shift)).astype(BF16)
                    dsc = (pc * (dpc - shift)).astype(BF16)
                    dq_ref[:, hs] = ((_nn(dsp, kpv) + _nn(dsc, kcv)) * 0.125).astype(BF16)
                    dkp = dkp + _tn(dsp, qh)
                    dkc = dkc + _tn(dsc, qh)
                    dvp = dvp + _tn(pp.astype(BF16), doh)
                    dvc = dvc + _tn(pc.astype(BF16), doh)
                    if has_sink:
                        psink = jnp.exp(sink_ref[0, h] - lse)
                        ds_ref[h:h + 1, :] += jnp.broadcast_to(-jnp.sum(psink * delta, axis=0, keepdims=True), (1, 128))
                dk_ref[:, ks] = (ck_ref[:, ks] + dkp).astype(BF16)
                dv_ref[:, ks] = (cv_ref[:, ks] + dvp).astype(BF16)
                ck_ref[:, ks] = dkc
                cv_ref[:, ks] = dvc

        @pl.when(n == nb)
        def _():
            dk_ref[...] = ck_ref[...].astype(BF16)
            dv_ref[...] = cv_ref[...].astype(BF16)

    q_spec, kp, kc, vp, vc = _attn_specs(dil, n_heads, n_kv, q_off, k_off, v_off, nb, total_w)
    qo_spec = pl.BlockSpec((BLK, qw), lambda r, n: (jnp.minimum(n, nb - 1), r))
    ko_spec = pl.BlockSpec((BLK, kw), lambda r, n: (jnp.maximum(n - 1, 0), r))
    in_specs = [q_spec, kp, kc, vp, vc, qo_spec, qo_spec]
    args = [qkv_f, qkv_f, qkv_f, qkv_f, qkv_f, do_f, lse_f]
    if has_dlse:
        in_specs.append(qo_spec)
        args.append(dlse_f)
    in_specs.append(pl.BlockSpec(bias.shape, lambda r, n: (0, 0, 0, 0)))
    args.append(bias)
    if has_sink:
        in_specs = [pl.BlockSpec(memory_space=pltpu.SMEM)] + in_specs
        args = [sinks] + args
    return pl.pallas_call(
        body, grid=(dil, nb + 1), in_specs=in_specs,
        out_specs=[qo_spec, ko_spec, ko_spec, pl.BlockSpec((8, 128), lambda r, n: (0, 0))],
        out_shape=[jax.ShapeDtypeStruct((seq, dil * qw), BF16), jax.ShapeDtypeStruct((seq, dil * kw), BF16),
                   jax.ShapeDtypeStruct((seq, dil * kw), BF16), jax.ShapeDtypeStruct((8, 128), F32)],
        scratch_shapes=[pltpu.VMEM((BLK, kw), F32), pltpu.VMEM((BLK, kw), F32)],
        compiler_params=_cp(("arbitrary", "arbitrary")), name=name,
    )(*args)


def _alibi_slopes():
    return jnp.exp2(-8.0 * jnp.arange(1, N_HEADS + 1, dtype=F32) / N_HEADS)


def _fold(v, dil):
    s_len, w = v.shape
    return v.reshape(s_len // dil, dil * w)


def _unfold(v, dil):
    seq, w = v.shape
    return v.reshape(seq * dil, w // dil)


def _b_offsets(g):
    q0 = A_QW + 2 * A_KW
    return q0 + g * B_OW, q0 + B_W + g * B_OW, q0 + 2 * B_W + g * B_OW


def _layer_forward(x_in, u1, modv, lnv, mod_next, sinks_l, wts, tabs, tm):
    w_in, w_a, w_b, w_o, w_g, w_u, w_d = wts
    tm_in = 1024 if u1.shape[0] % 1024 == 0 else tm
    qkv = _in_proj(u1, w_in, 0, QKV_W // 256, 256, BF16, tm_in, "in_proj_qkv")
    gates = _in_proj(u1, w_in, QKV_W // 256, 2 * D // 256, 256, F32, tm_in, "in_proj_gates")
    ya, lse_a = _attn_forward(qkv, tabs[0], sinks_l, dil=1, n_heads=A_H, n_kv=A_HKV, q_off=0, k_off=A_QW,
                              v_off=A_QW + A_KW, out_dtype=BF16, name="attn_a_forward")
    o_g, lse_g = [], []
    for g, (_, dil) in enumerate(B_GROUPS):
        qo, ko, vo = _b_offsets(g)
        o, l = _attn_forward(_fold(qkv, dil), tabs[1 + g], None, dil=dil, n_heads=B_HG, n_kv=B_HG,
                             q_off=qo, k_off=ko, v_off=vo, out_dtype=F32, name="attn_b%d_forward" % g)
        o_g.append(_unfold(o, dil))
        lse_g.append(_unfold(l, dil))
    x_mid, xhat1, rstd1, u2, t1, yb = _mix_out(ya, o_g, lse_g, gates, x_in, modv, lnv, w_a, w_b, w_o, tm)
    tm_ffn = 512 if u2.shape[0] % 512 == 0 else tm
    a, b, t2, xhat2, rstd2, x_out, u_next = _ffn_forward(u2, x_mid, modv, lnv, mod_next, w_g, w_u, w_d, tm_ffn)
    saved = dict(x_in=x_in, u1=u1, qkv=qkv, gates=gates, ya=ya, lse_a=lse_a, o_g=o_g, lse_g=lse_g, yb=yb,
                 x_mid=x_mid, xhat1=xhat1, rstd1=rstd1, u2=u2, t1=t1, a=a, b=b, t2=t2, xhat2=xhat2, rstd2=rstd2)
    return x_out, u_next, saved


def _layer_backward(dxo, sv, modv, lnv, sinks_l, wts, tabs, tm):
    w_in, w_a, w_b, w_o, w_g, w_u, w_d = wts
    s_len = dxo.shape[0]
    ts = 512 if s_len % 512 == 0 else s_len
    dxm, dt2, da, db, f, small2 = _ffn_backward(dxo, sv["xhat2"], sv["rstd2"], sv["t2"], sv["x_mid"], sv["a"],
                                                sv["b"], modv, lnv, w_g, w_u, w_d, ts)
    hid_a = dict(a_block=(ts, D), a_map=lambda p, s: (s, 0), b_block=(None, ts, FF_S), b_map=lambda p, s: (p, s, 0),
                 out_shape=(N_CHIPS, D, FF_S), out_block=(None, D, FF_S), out_map=lambda p, s: (p, 0, 0),
                 n_panels=N_CHIPS, s_len=s_len, ts=ts)
    dw_g = _weight_grad(sv["u2"], da, name="dw_gate", **hid_a)
    dw_u = _weight_grad(sv["u2"], db, name="dw_up", **hid_a)
    dw_d = _weight_grad(f, dt2, a_block=(None, ts, FF_S), a_map=lambda p, s: (p, s, 0), b_block=(ts, D),
                        b_map=lambda p, s: (s, 0), out_shape=(N_CHIPS, FF_S, D), out_block=(None, FF_S, D),
                        out_map=lambda p, s: (p, 0, 0), n_panels=N_CHIPS, s_len=s_len, ts=ts, name="dw_down")
    dxi, dt1, merged, dza, dzb, dgates, dya, dyb, small1 = _mix_backward(
        dxm, sv["xhat1"], sv["rstd1"], sv["t1"], sv["gates"], sv["ya"], sv["yb"], modv, lnv, w_a, w_b, w_o, tm)
    rw = D // N_CHIPS
    dw_o = _weight_grad(merged, dt1, a_block=(ts, rw), a_map=lambda p, s: (s, p), b_block=(ts, D),
                        b_map=lambda p, s: (s, 0), out_shape=(N_CHIPS, rw, D), out_block=(None, rw, D),
                        out_map=lambda p, s: (p, 0, 0), n_panels=N_CHIPS, s_len=s_len, ts=ts, name="dw_o")
    col = dict(b_block=(ts, rw), b_map=lambda p, s: (s, p), n_panels=N_CHIPS, s_len=s_len, ts=ts)
    dw_a = _weight_grad(sv["ya"], dza, a_block=(ts, A_QW), a_map=lambda p, s: (s, 0), out_shape=(N_CHIPS, A_QW, rw),
                        out_block=(None, A_QW, rw), out_map=lambda p, s: (p, 0, 0), name="dw_a", **col)
    dw_b = _weight_grad(sv["yb"], dzb, a_block=(ts, B_OW), a_map=lambda p, s: (s, 0), out_shape=(N_CHIPS, B_OW, rw),
                        out_block=(None, B_OW, rw), out_map=lambda p, s: (p, 0, 0), name="dw_b", **col)
    qkv = sv["qkv"]
    dqa, dka, dva, dsink = _attn_backward(qkv, dya, sv["lse_a"], None, tabs[0], sinks_l, dil=1, n_heads=A_H,
                                          n_kv=A_HKV, q_off=0, k_off=A_QW, v_off=A_QW + A_KW, name="attn_a_backward")
    do_g = _group_mix_backward(dyb, sv["o_g"], sv["lse_g"], tm)
    dq_b, dk_b, dv_b = [], [], []
    for g, (_, dil) in enumerate(B_GROUPS):
        qo, ko, vo = _b_offsets(g)
        dq, dk, dv, _ = _attn_backward(_fold(qkv, dil), _fold(do_g[g], dil), _fold(sv["lse_g"][g], dil),
                                       _fold(do_g[3 + g], dil), tabs[1 + g], None, dil=dil, n_heads=B_HG, n_kv=B_HG,
                                       q_off=qo, k_off=ko, v_off=vo, name="attn_b%d_backward" % g)
        dq_b.append(_unfold(dq, dil))
        dk_b.append(_unfold(dk, dil))
        dv_b.append(_unfold(dv, dil))
    dh = jnp.concatenate([dqa, dka, dva] + dq_b + dk_b + dv_b + [dgates], axis=1)
    dw_in = _weight_grad(sv["u1"], dh, a_block=(ts, D), a_map=lambda p, s: (s, 0), b_block=(ts, IN_S),
                         b_map=lambda p, s: (s, p), out_shape=(N_CHIPS, D, IN_S), out_block=(None, D, IN_S),
                         out_map=lambda p, s: (p, 0, 0), n_panels=N_CHIPS, s_len=s_len, ts=ts, name="dw_in")
    dx_in, small0 = _in_proj_backward(dh, w_in, dxi, sv["x_in"], modv, ts)
    dmod = jnp.stack([small0[1], small0[0], small1[2], small2[4], small2[3], small2[2]])
    dln = jnp.stack([small1[0], small1[1], small2[0], small2[1]])
    grads = (dw_in, dw_a, dw_b, dw_o, dw_g, dw_u, dw_d)
    return dx_in, grads, dmod, dln, dsink[:, 0]


def _local_step(x, target, mod, sinks, ln, gathered, tm):
    slopes = _alibi_slopes()
    tabs = [_bias_tables(slopes[:A_H], A_WINDOW - 1, 1)]
    for g, (window, dil) in enumerate(B_GROUPS):
        lo = A_H + g * B_HG
        tabs.append(_bias_tables(slopes[lo:lo + B_HG], window // dil, dil))
    zeros2 = jnp.zeros((2, D), F32)
    modvs = [jnp.concatenate([mod[l].reshape(6, D), zeros2]) for l in range(DEPTH)]
    lnvs = [jnp.concatenate([jnp.stack([ln[0][l], ln[1][l], ln[2][l], ln[3][l]]), jnp.zeros((4, D), F32)])
            for l in range(DEPTH)]
    modvs.append(jnp.zeros((8, D), F32))
    u = _modulate(x, modvs[0], 1, 0, tm)
    saved = []
    for l in range(DEPTH):
        x, u, sv = _layer_forward(x, u, modvs[l], lnvs[l], modvs[l + 1], sinks[l:l + 1], gathered[l], tabs, tm)
        saved.append(sv)
    dx, loss = _loss_head(x, target, tm)
    grads, dmods, dlns, dsinks = [None] * DEPTH, [None] * DEPTH, [None] * DEPTH, [None] * DEPTH
    for l in reversed(range(DEPTH)):
        dx, grads[l], dmods[l], dlns[l], dsinks[l] = _layer_backward(
            dx, saved[l], modvs[l], lnvs[l], sinks[l:l + 1], gathered[l], tabs, tm)
    return loss, dx, grads, jnp.stack(dmods), jnp.stack(dlns), jnp.stack(dsinks)


def _pack_small(b_ada_like, ln1_g, ln1_b, ln2_g, ln2_b, sinks_like, loss_row):
    sink_row = jnp.zeros((D,), F32).at[:DEPTH * A_H].set(sinks_like.reshape(-1))
    rows = [b_ada_like.reshape(DEPTH * 6, D), ln1_g, ln1_b, ln2_g, ln2_b, sink_row[None], loss_row[None],
            jnp.zeros((SMALL_ROWS - DEPTH * 10 - 2, D), F32)]
    return jnp.concatenate(rows)


def _unpack_small(p):
    n0 = DEPTH * 6
    return (p[:n0].reshape(DEPTH, 6 * D), p[n0:n0 + 4], p[n0 + 4:n0 + 8], p[n0 + 8:n0 + 12], p[n0 + 12:n0 + 16],
            p[n0 + 16, :DEPTH * A_H].reshape(DEPTH, A_H))


def kernel(x, c, w_ada, b_ada, w_in, sinks, w_a, w_b, w_o, ln1_g, ln1_b, w_gate, w_up, w_down, ln2_g, ln2_b, loss_target, m_w_ada, m_b_ada, m_w_in, m_sinks, m_w_a, m_w_b, m_w_o, m_ln1_g, m_ln1_b, m_w_gate, m_w_up, m_w_down, m_ln2_g, m_ln2_b, v_w_ada, v_b_ada, v_w_in, v_sinks, v_w_a, v_w_b, v_w_o, v_ln1_g, v_ln1_b, v_w_gate, v_w_up, v_w_down, v_ln2_g, v_ln2_b):
    s_len = x.shape[1]
    tm = 256
    xi, yi, ci = _my_pos()
    chip = 2 * xi + yi
    dev = 4 * xi + 2 * yi + ci
    n_ada = w_ada.shape[2]

    c_all = _all_gather_small(jnp.concatenate([c, jnp.zeros((7, D), F32)]))[:, 0]
    b_cols = lax.dynamic_slice_in_dim(b_ada, chip * n_ada, n_ada, axis=1)
    mod_cols = _ada_forward(c_all, w_ada, jnp.broadcast_to(b_cols[:, None, :], (DEPTH, N_DEV, n_ada)))
    mod_all = _all_gather_small(mod_cols.reshape(DEPTH * N_DEV, n_ada))
    mod_all = mod_all.reshape(N_CHIPS, 2, DEPTH, N_DEV, n_ada)[:, 0]
    mod_mine = lax.dynamic_index_in_dim(mod_all, dev, axis=2, keepdims=False)
    mod = jnp.transpose(mod_mine, (1, 0, 2)).reshape(DEPTH, N_CHIPS * n_ada)

    big = (w_in, w_a, w_b, w_o, w_gate, w_up, w_down)
    gathered = []
    for l in range(DEPTH):
        g_in, g_a, g_b, g_o, g_g, g_u, g_d = _gather_weights([w[l].astype(BF16) for w in big])
        gathered.append((g_in, g_a, g_b, g_o.reshape(D, D), g_g, g_u, g_d))

    loss_blk, grad_x, grads, dmod, dln, dsinks = _local_step(
        x[0], loss_target[0], mod, sinks, (ln1_g, ln1_b, ln2_g, ln2_b), gathered, tm)

    stacked = [jnp.stack([grads[l][w] for l in range(DEPTH)]) for w in range(len(big))]
    recv = _scatter_grads(stacked)
    part = [_sum_slots(r.reshape(N_CHIPS, -1, r.shape[-1])) for r in recv]
    other = _swap_with_sibling(part)
    big_m = (m_w_in, m_w_a, m_w_b, m_w_o, m_w_gate, m_w_up, m_w_down)
    big_v = (v_w_in, v_w_a, v_w_b, v_w_o, v_w_gate, v_w_up, v_w_down)
    big_out = []
    for w, mm, vv, p, q in zip(big, big_m, big_v, part, other):
        shp = w.shape
        flat = lambda t: t.reshape(-1, shp[-1])
        res = _adam(flat(w), [p, q], flat(mm), flat(vv))
        big_out.append([t.reshape(shp) for t in res])

    small = _pack_small(dmod.reshape(DEPTH, 6 * D), dln[:, 0], dln[:, 1], dln[:, 2], dln[:, 3], dsinks, loss_blk[0, :1].repeat(D))
    small_all = _all_gather_small(small)
    sg, sd, sm, sv = _small_reduce_adam(
        small_all,
        _pack_small(b_ada, ln1_g, ln1_b, ln2_g, ln2_b, sinks, jnp.zeros((D,), F32)),
        _pack_small(m_b_ada, m_ln1_g, m_ln1_b, m_ln2_g, m_ln2_b, m_sinks, jnp.zeros((D,), F32)),
        _pack_small(v_b_ada, v_ln1_g, v_ln1_b, v_ln2_g, v_ln2_b, v_sinks, jnp.zeros((D,), F32)))
    loss = sg[DEPTH * 10 + 1, 0]
    g_small, d_small, m_small, v_small = _unpack_small(sg), _unpack_small(sd), _unpack_small(sm), _unpack_small(sv)

    dmod_all = small_all[:, :DEPTH * 6].reshape(N_DEV, DEPTH, 6 * D)
    dmod_cols = lax.dynamic_slice_in_dim(dmod_all, chip * n_ada, n_ada, axis=2)
    dmod_pad = jnp.concatenate([jnp.transpose(dmod_cols, (1, 0, 2)), jnp.zeros((DEPTH, 8, n_ada), F32)], axis=1)
    c_pad = jnp.concatenate([c_all, jnp.zeros((8, D), F32)])
    g_ada = _ada_backward(c_pad, dmod_pad)
    flat_ada = lambda t: t.reshape(-1, n_ada)
    ada_out = [t.reshape(w_ada.shape) for t in _adam(flat_ada(w_ada), [flat_ada(g_ada)], flat_ada(m_w_ada), flat_ada(v_w_ada))]

    def ordered(k):
        sm_k = (g_small, d_small, m_small, v_small)[k]
        bg = [o[k] for o in big_out]
        return [ada_out[k], sm_k[0], bg[0], sm_k[5], bg[1], bg[2], bg[3], sm_k[1], sm_k[2], bg[4], bg[5], bg[6],
                sm_k[3], sm_k[4]]

    return (loss, grad_x[None], *ordered(0), *ordered(1), *ordered(2), *ordered(3))
```
